```python
import math
import jax, jax.numpy as jnp
from jax import lax
import numpy as np

D_MODEL = 1024
BATCH = 8
SEQ = 2048
DEPTH = 4
DEC_BATCH = 128
DEC_SEQ = 4
PAST_LEN = 2048
PAGE_SIZE = 128

HD = 64
ROT_DIM = HD // 4
ROPE_THETA = 500000.0
MIX_W = D_MODEL
A_HEADS = (MIX_W // 2) // (2 * HD)
B_HEADS = (MIX_W // 2) // HD
C_CH = MIX_W // 2
C_GROUPS = 4
C_GW = C_CH // C_GROUPS
CHUNK = 128
D_HEADS = (MIX_W // 2) // HD
D_KV = 2
D_REP = D_HEADS // D_KV
CMP_BLOCK = 32
SEL_BLOCK = 64
N_SEL = 8
WINDOW = 512
D_FF = ((8 * D_MODEL // 3 + 127) // 128) * 128
N_EVEN = (DEPTH + 1) // 2
N_ODD = DEPTH // 2
Q_BLOCK = 128
EPS = 1e-6
NEG = -1e30
TINY = 1e-30
FOX_GATE_BIAS_INIT = 3.0
EVEN_SIZES = (A_HEADS * 2 * HD, A_HEADS * 2 * HD, A_HEADS * 2 * HD, B_HEADS * HD, B_HEADS * HD, B_HEADS * HD, B_HEADS)
EVEN_COLS = sum(EVEN_SIZES)
ODD_SIZES = (C_CH, C_CH, D_HEADS * HD) + (D_KV * HD,) * 6 + (3 * D_HEADS,)
ODD_COLS = sum(ODD_SIZES)

kernel_name = 'hybrid_diff_fox_gmlp_nsa_decoder_step'


def rmsnorm(x, g):
    xf = x.astype(jnp.float32)
    y = xf * lax.rsqrt(jnp.mean(xf * xf, axis=-1, keepdims=True) + EPS)
    return (y * g.astype(jnp.float32)).astype(x.dtype)


def layernorm(x, g, b):
    xf = x.astype(jnp.float32)
    mu = jnp.mean(xf, axis=-1, keepdims=True)
    var = jnp.mean(jnp.square(xf - mu), axis=-1, keepdims=True)
    y = (xf - mu) * lax.rsqrt(var + EPS) * g.astype(jnp.float32) + b.astype(jnp.float32)
    return y.astype(x.dtype)


def rope_partial(x, pos):
    half = ROT_DIM // 2
    inv = 1.0 / (ROPE_THETA ** (jnp.arange(half, dtype=jnp.float32) * 2.0 / ROT_DIM))
    ang = pos.astype(jnp.float32)[:, None] * inv[None, :]
    shape = (1, pos.shape[0]) + (1,) * (x.ndim - 3) + (half,)
    cos = jnp.cos(ang).reshape(shape)
    sin = jnp.sin(ang).reshape(shape)
    xf = x.astype(jnp.float32)
    x1 = xf[..., :half]
    x2 = xf[..., half:ROT_DIM]
    out = jnp.concatenate([x1 * cos - x2 * sin, x2 * cos + x1 * sin, xf[..., ROT_DIM:]], axis=-1)
    return out.astype(x.dtype)


def masked_softmax(s, mask):
    s = jnp.where(mask, s, NEG)
    m = jnp.max(s, axis=-1, keepdims=True)
    p = jnp.where(mask, jnp.exp(s - m), 0.0)
    return p / jnp.maximum(jnp.sum(p, axis=-1, keepdims=True), TINY)


def split_cols(z, sizes):
    offs = np.cumsum(np.array(sizes))[:-1].tolist()
    return jnp.split(z, offs, axis=-1)


def to_blocks(x):
    b, s = x.shape[:2]
    return jnp.moveaxis(x.reshape((b, s // Q_BLOCK, Q_BLOCK) + x.shape[2:]), 1, 0)


def from_blocks(y):
    nb, b, qb = y.shape[:3]
    return jnp.moveaxis(y, 0, 1).reshape((b, nb * qb) + y.shape[3:])


def gather_pages(pool, layer, page_table):
    g = pool[layer, page_table]
    return g.reshape((g.shape[0], g.shape[1] * g.shape[2]) + g.shape[3:])


def swiglu(x, w_in, w_out):
    gate, up = jnp.split(x @ w_in, 2, axis=-1)
    return (jax.nn.silu(gate) * up) @ w_out


def ffn_half(x, g_pre, g_post, w_in, w_out):
    return x + 0.5 * rmsnorm(swiglu(rmsnorm(x, g_pre), w_in, w_out), g_post)


def even_project(h, pos, w_in, b_f):
    b, t, _ = h.shape
    aq, ak, av, bq, bk, bv, bf = split_cols(h @ w_in, EVEN_SIZES)
    aq = rope_partial(aq.reshape(b, t, A_HEADS, 2, HD), pos)
    ak = rope_partial(ak.reshape(b, t, A_HEADS, 2, HD), pos)
    av = av.reshape(b, t, A_HEADS, 2 * HD)
    bq = bq.reshape(b, t, B_HEADS, HD)
    bk = bk.reshape(b, t, B_HEADS, HD)
    bv = bv.reshape(b, t, B_HEADS, HD)
    logf = jax.nn.log_sigmoid((bf + b_f).astype(jnp.float32))
    return aq, ak, av, bq, bk, bv, logf


def diff_lambda(lam_p, lam_init):
    lp = lam_p.astype(jnp.float32)
    return jnp.exp(jnp.sum(lp[0] * lp[1])) - jnp.exp(jnp.sum(lp[2] * lp[3])) + lam_init


def diff_attn(q, k, v, q_pos, k_pos, lam):
    s = jnp.einsum('bqhcd,bthcd->bhcqt', q, k).astype(jnp.float32) * (HD ** -0.5)
    mask = k_pos[None, :] <= q_pos[:, None]
    p = masked_softmax(s, mask)
    a = p[:, :, 0] - lam * p[:, :, 1]
    return jnp.einsum('bhqt,bthe->bqhe', a.astype(v.dtype), v)


def fox_attn(q, k, v, q_pos, k_pos, fq, fk):
    s = jnp.einsum('bqhd,bthd->bhqt', q, k).astype(jnp.float32) * (HD ** -0.5)
    s = s + jnp.transpose(fq, (0, 2, 1))[:, :, :, None] - jnp.transpose(fk, (0, 2, 1))[:, :, None, :]
    mask = k_pos[None, :] <= q_pos[:, None]
    p = masked_softmax(s, mask)
    return jnp.einsum('bhqt,bthd->bqhd', p.astype(v.dtype), v)


def even_merge(oa, ob, subln_g, lam_init, w_out):
    b, t = oa.shape[:2]
    oa = rmsnorm(oa, subln_g) * (1.0 - lam_init)
    o = jnp.concatenate([oa.reshape(b, t, -1), ob.reshape(b, t, -1).astype(oa.dtype)], axis=-1)
    return o @ w_out


def even_prompt(h, w_in, b_f, lam_p, subln_g, w_out, lam_init):
    b, s, _ = h.shape
    pos = jnp.arange(s)
    aq, ak, av, bq, bk, bv, logf = even_project(h, pos, w_in, b_f)
    lam = diff_lambda(lam_p, lam_init)
    f_cum = jnp.cumsum(logf, axis=1)

    def blk(args):
        qa, qb, fq, start = args
        qpos = start + jnp.arange(Q_BLOCK)
        return (diff_attn(qa, ak, av, qpos, pos, lam), fox_attn(qb, bk, bv, qpos, pos, fq, f_cum))

    starts = jnp.arange(s // Q_BLOCK) * Q_BLOCK
    oa, ob = lax.map(blk, (to_blocks(aq), to_blocks(bq), to_blocks(f_cum), starts))
    y = even_merge(from_blocks(oa), from_blocks(ob), subln_g, lam_init, w_out)
    diff_kv = jnp.stack([ak.reshape(b, s, A_HEADS, 2 * HD), av], axis=2)
    fox_kv = jnp.stack([bk, bv], axis=2)
    return y, diff_kv, fox_kv, logf.astype(h.dtype)


def even_sample(h, layer, cache_diff_kv, cache_fox_kv, cache_fox_logf, page_table,
                w_in, b_f, lam_p, subln_g, w_out, lam_init):
    b, t, _ = h.shape
    pos = PAST_LEN + jnp.arange(t)
    aq, ak, av, bq, bk, bv, logf = even_project(h, pos, w_in, b_f)
    lam = diff_lambda(lam_p, lam_init)
    pd = gather_pages(cache_diff_kv, layer, page_table)
    pf = gather_pages(cache_fox_kv, layer, page_table)
    pl = gather_pages(cache_fox_logf, layer, page_table)
    k_a = jnp.concatenate([pd[:, :, 0].reshape(b, -1, A_HEADS, 2, HD), ak], axis=1)
    v_a = jnp.concatenate([pd[:, :, 1], av], axis=1)
    k_pos = jnp.arange(k_a.shape[1])
    oa = diff_attn(aq, k_a, v_a, pos, k_pos, lam)
    f_cum = jnp.cumsum(jnp.concatenate([pl.astype(jnp.float32), logf], axis=1), axis=1)
    k_b = jnp.concatenate([pf[:, :, 0], bk], axis=1)
    v_b = jnp.concatenate([pf[:, :, 1], bv], axis=1)
    ob = fox_attn(bq, k_b, v_b, pos, k_pos, f_cum[:, -t:], f_cum)
    y = even_merge(oa, ob, subln_g, lam_init, w_out)
    diff_kv = jnp.stack([ak.reshape(b, t, A_HEADS, 2 * HD), av], axis=2)
    fox_kv = jnp.stack([bk, bv], axis=2)
    return y, diff_kv, fox_kv, logf.astype(h.dtype)


def odd_project(h, pos, w_in, ln_g, ln_b):
    b, t, _ = h.shape
    u, v, q, ck, cv, sk, sv, wk, wv, g = split_cols(h @ w_in, ODD_SIZES)
    u = jax.nn.gelu(u).reshape(b, t, C_GROUPS, C_GW)
    v = layernorm(jax.nn.gelu(v).reshape(b, t, C_GROUPS, C_GW),
                  ln_g.reshape(C_GROUPS, C_GW), ln_b.reshape(C_GROUPS, C_GW))
    q = q.reshape(b, t, D_HEADS, HD)
    q_rot = rope_partial(q, pos)
    kv = lambda a: a.reshape(b, t, D_KV, HD)
    ck, cv, sv, wv = kv(ck), kv(cv), kv(sv), kv(wv)
    sk = rope_partial(kv(sk), pos)
    wk = rope_partial(kv(wk), pos)
    gates = jax.nn.sigmoid(g.astype(jnp.float32)).reshape(b, t, D_HEADS, 3)
    return u, v, q, q_rot, ck, cv, sk, sv, wk, wv, gates


def nsa_blocks(ck, cv, sk, sv):
    b, t = ck.shape[:2]
    tp = ((t + SEL_BLOCK - 1) // SEL_BLOCK) * SEL_BLOCK
    padf = lambda a: jnp.pad(a, ((0, 0), (0, tp - t), (0, 0), (0, 0)))
    ck, cv, sk, sv = padf(ck), padf(cv), padf(sk), padf(sv)
    nc = tp // CMP_BLOCK
    ns = tp // SEL_BLOCK
    kc = jnp.mean(ck.reshape(b, nc, CMP_BLOCK, D_KV, HD).astype(jnp.float32), axis=2).astype(ck.dtype)
    vc = jnp.mean(cv.reshape(b, nc, CMP_BLOCK, D_KV, HD).astype(jnp.float32), axis=2).astype(cv.dtype)
    skb = sk.reshape(b, ns, SEL_BLOCK, D_KV, HD).transpose(0, 3, 1, 2, 4)
    svb = sv.reshape(b, ns, SEL_BLOCK, D_KV, HD).transpose(0, 3, 1, 2, 4)
    return kc, vc, skb, svb


def nsa_core(q, q_rot, gates, q_pos, kc, vc, skb, svb, wk, wv, w_pos):
    b, nq = q.shape[:2]
    scale = HD ** -0.5
    qg = q.reshape(b, nq, D_KV, D_REP, HD)
    qr = q_rot.reshape(b, nq, D_KV, D_REP, HD)
    nc = kc.shape[1]
    ns = skb.shape[2]
    c_end = jnp.arange(nc) * CMP_BLOCK + (CMP_BLOCK - 1)
    cmask = c_end[None, :] <= q_pos[:, None]
    sc = jnp.einsum('bqgrd,bngd->bgrqn', qg, kc).astype(jnp.float32) * scale
    pc = masked_softmax(sc, cmask)
    o_c = jnp.einsum('bgrqn,bngd->bqgrd', pc.astype(vc.dtype), vc)
    imp = jnp.sum(pc, axis=2).reshape(b, D_KV, nq, ns, SEL_BLOCK // CMP_BLOCK).sum(-1)
    blk_id = jnp.arange(ns)
    own = (q_pos // SEL_BLOCK)[:, None] == blk_id[None, :]
    valid = (blk_id * SEL_BLOCK)[None, :] <= q_pos[:, None]
    score = jnp.where(own, D_REP + 1.0, jnp.where(valid, imp, -1.0))
    _, idx = lax.top_k(score, min(N_SEL, ns))
    bi = jnp.arange(b)[:, None, None, None]
    gi = jnp.arange(D_KV)[None, :, None, None]
    kg = skb[bi, gi, idx]
    vg = svb[bi, gi, idx]
    nsel = idx.shape[-1]
    kpos = idx[..., None] * SEL_BLOCK + jnp.arange(SEL_BLOCK)
    smask = (kpos <= q_pos[None, None, :, None, None]).reshape(b, D_KV, 1, nq, nsel * SEL_BLOCK)
    ss = jnp.einsum('bqgrd,bgqnld->bgrqnl', qr, kg).astype(jnp.float32) * scale
    ps = masked_softmax(ss.reshape(b, D_KV, D_REP, nq, nsel * SEL_BLOCK), smask)
    o_s = jnp.einsum('bgrqk,bgqkd->bqgrd', ps.astype(vg.dtype), vg.reshape(b, D_KV, nq, nsel * SEL_BLOCK, HD))
    wmask = ((w_pos[None, :] <= q_pos[:, None]) & (w_pos[None, :] > q_pos[:, None] - WINDOW)
             & (w_pos[None, :] >= 0))
    sw = jnp.einsum('bqgrd,btgd->bgrqt', qr, wk).astype(jnp.float32) * scale
    pw = masked_softmax(sw, wmask)
    o_w = jnp.einsum('bgrqt,btgd->bqgrd', pw.astype(wv.dtype), wv)
    g = gates.reshape(b, nq, D_KV, D_REP, 3)
    o = g[..., 0:1] * o_c + g[..., 1:2] * o_s + g[..., 2:3] * o_w
    return o.reshape(b, nq, D_HEADS, HD)


def odd_prompt(h, w_in, ln_g, ln_b, w_s, b_s, w_out):
    b, s, _ = h.shape
    pos = jnp.arange(s)
    u, v, q, q_rot, ck, cv, sk, sv, wk, wv, gates = odd_project(h, pos, w_in, ln_g, ln_b)
    wm = jnp.tril(w_s)
    nch = s // CHUNK
    vc_ = v.reshape(b, nch, CHUNK, C_GROUPS, C_GW)
    mix = jnp.einsum('gts,bnsgc->bntgc', wm, vc_) + b_s.T[None, None, :, :, None]
    oc = (u.reshape(b, nch, CHUNK, C_GROUPS, C_GW) * mix).reshape(b, s, C_CH)
    kc, vc, skb, svb = nsa_blocks(ck, cv, sk, sv)
    wk_pad = jnp.pad(wk, ((0, 0), (WINDOW, 0), (0, 0), (0, 0)))
    wv_pad = jnp.pad(wv, ((0, 0), (WINDOW, 0), (0, 0), (0, 0)))

    def blk(args):
        qb, qrb, gb, start = args
        qpos = start + jnp.arange(Q_BLOCK)
        wkb = lax.dynamic_slice_in_dim(wk_pad, start, WINDOW + Q_BLOCK, axis=1)
        wvb = lax.dynamic_slice_in_dim(wv_pad, start, WINDOW + Q_BLOCK, axis=1)
        wpos = start - WINDOW + jnp.arange(WINDOW + Q_BLOCK)
        return nsa_core(qb, qrb, gb, qpos, kc, vc, skb, svb, wkb, wvb, wpos)

    starts = jnp.arange(s // Q_BLOCK) * Q_BLOCK
    od = from_blocks(lax.map(blk, (to_blocks(q), to_blocks(q_rot), to_blocks(gates), starts)))
    y = jnp.concatenate([oc, od.reshape(b, s, -1).astype(oc.dtype)], axis=-1) @ w_out
    cmp_kv = jnp.stack([ck, cv], axis=2)
    sel_kv = jnp.stack([sk, sv], axis=2)
    win_kv = jnp.stack([wk, wv], axis=2)[:, s - min(WINDOW, s):]
    return y, cmp_kv, sel_kv, win_kv


def odd_sample(h, layer, cache_nsa_cmp_kv, cache_nsa_sel_kv, state_nsa_win_kv, page_table,
               w_in, ln_g, ln_b, w_s, b_s, w_out):
    b, t, _ = h.shape
    pos = PAST_LEN + jnp.arange(t)
    u, v, q, q_rot, ck, cv, sk, sv, wk, wv, gates = odd_project(h, pos, w_in, ln_g, ln_b)
    wm = jnp.tril(w_s)[:, :t, :t]
    mix = jnp.einsum('gts,bsgc->btgc', wm, v) + b_s[:, :t].T[None, :, :, None]
    oc = (u * mix).reshape(b, t, C_CH)
    pc = gather_pages(cache_nsa_cmp_kv, layer, page_table)
    psel = gather_pages(cache_nsa_sel_kv, layer, page_table)
    ck_all = jnp.concatenate([pc[:, :, 0], ck], axis=1)
    cv_all = jnp.concatenate([pc[:, :, 1], cv], axis=1)
    sk_all = jnp.concatenate([psel[:, :, 0], sk], axis=1)
    sv_all = jnp.concatenate([psel[:, :, 1], sv], axis=1)
    kc, vc, skb, svb = nsa_blocks(ck_all, cv_all, sk_all, sv_all)
    buf = state_nsa_win_kv[layer]
    wb = buf.shape[1]
    wk_all = jnp.concatenate([buf[:, :, 0], wk], axis=1)
    wv_all = jnp.concatenate([buf[:, :, 1], wv], axis=1)
    wpos = PAST_LEN - wb + jnp.arange(wb + t)
    od = nsa_core(q, q_rot, gates, pos, kc, vc, skb, svb, wk_all, wv_all, wpos)
    y = jnp.concatenate([oc, od.reshape(b, t, -1).astype(oc.dtype)], axis=-1) @ w_out
    cmp_kv = jnp.stack([ck, cv], axis=2)
    sel_kv = jnp.stack([sk, sv], axis=2)
    win_kv = jnp.stack([wk, wv], axis=2)
    return y, cmp_kv, sel_kv, win_kv, v.reshape(b, t, C_CH)


def setup_inputs(seed: int = 0) -> dict:
    key = jax.random.key(seed)
    ks = jax.random.split(key, 32)
    f32 = jnp.float32
    nrm = lambda k, shape, sc: sc * jax.random.normal(k, shape, f32)
    n_pages = PAST_LEN // PAGE_SIZE
    n_used = DEC_BATCH * n_pages
    n_pool = n_used + n_used // 4
    wb = min(WINDOW, PAST_LEN)
    return {
        'x_prompt': nrm(ks[0], (BATCH, SEQ, D_MODEL), 1.0),
        'x_sample': nrm(ks[1], (DEC_BATCH, DEC_SEQ, D_MODEL), 1.0),
        'cache_diff_kv': nrm(ks[2], (N_EVEN, n_pool, PAGE_SIZE, 2, A_HEADS, 2 * HD), 1.0),
        'cache_fox_kv': nrm(ks[3], (N_EVEN, n_pool, PAGE_SIZE, 2, B_HEADS, HD), 1.0),
        'cache_fox_logf': jax.nn.log_sigmoid(FOX_GATE_BIAS_INIT + jax.random.normal(ks[4], (N_EVEN, n_pool, PAGE_SIZE, B_HEADS), f32)),
        'cache_nsa_cmp_kv': nrm(ks[5], (N_ODD, n_pool, PAGE_SIZE, 2, D_KV, HD), 1.0),
        'cache_nsa_sel_kv': nrm(ks[6], (N_ODD, n_pool, PAGE_SIZE, 2, D_KV, HD), 1.0),
        'state_nsa_win_kv': nrm(ks[7], (N_ODD, DEC_BATCH, wb, 2, D_KV, HD), 1.0),
        'page_table': jax.random.permutation(ks[8], n_pool)[:n_used].reshape(DEC_BATCH, n_pages).astype(jnp.int32),
        'norm_g': 1.0 + nrm(ks[9], (DEPTH, 6, D_MODEL), 0.05),
        'ffn_w_in': nrm(ks[10], (DEPTH, 2, D_MODEL, 2 * D_FF), D_MODEL ** -0.5),
        'ffn_w_out': nrm(ks[11], (DEPTH, 2, D_FF, D_MODEL), D_FF ** -0.5),
        'even_w_in': nrm(ks[12], (N_EVEN, D_MODEL, EVEN_COLS), D_MODEL ** -0.5),
        'even_w_out': nrm(ks[13], (N_EVEN, MIX_W, D_MODEL), MIX_W ** -0.5),
        'fox_b_f': FOX_GATE_BIAS_INIT + nrm(ks[14], (N_EVEN, B_HEADS), 0.1),
        'diff_lambda': nrm(ks[15], (N_EVEN, 4, HD), 0.1),
        'diff_subln_g': 1.0 + nrm(ks[16], (N_EVEN, 2 * HD), 0.05),
        'odd_w_in': nrm(ks[17], (N_ODD, D_MODEL, ODD_COLS), D_MODEL ** -0.5),
        'odd_w_out': nrm(ks[18], (N_ODD, MIX_W, D_MODEL), MIX_W ** -0.5),
        'gmlp_ln_g': 1.0 + nrm(ks[19], (N_ODD, C_CH), 0.05),
        'gmlp_ln_b': nrm(ks[20], (N_ODD, C_CH), 0.02),
        'gmlp_w_s': nrm(ks[21], (N_ODD, C_GROUPS, CHUNK, CHUNK), CHUNK ** -0.5),
        'gmlp_b_s': 1.0 + nrm(ks[22], (N_ODD, C_GROUPS, CHUNK), 0.05),
    }


def reference(x_prompt, x_sample, cache_diff_kv, cache_fox_kv, cache_fox_logf, cache_nsa_cmp_kv,
              cache_nsa_sel_kv, state_nsa_win_kv, page_table, norm_g, ffn_w_in, ffn_w_out,
              even_w_in, even_w_out, fox_b_f, diff_lambda, diff_subln_g, odd_w_in, odd_w_out,
              gmlp_ln_g, gmlp_ln_b, gmlp_w_s, gmlp_b_s):
    yp, ys = x_prompt, x_sample
    dkv_p, dkv_s, fkv_p, fkv_s, fl_p, fl_s = [], [], [], [], [], []
    ckv_p, ckv_s, skv_p, skv_s, wkv_p, wkv_s, gv_s = [], [], [], [], [], [], []
    for i in range(DEPTH):
        nrm = norm_g[i]
        yp = ffn_half(yp, nrm[0], nrm[1], ffn_w_in[i, 0], ffn_w_out[i, 0])
        ys = ffn_half(ys, nrm[0], nrm[1], ffn_w_in[i, 0], ffn_w_out[i, 0])
        if i % 2 == 0:
            e = i // 2
            lam_init = 0.8 - 0.6 * math.exp(-0.3 * i)
            mp, a1, a2, a3 = even_prompt(rmsnorm(yp, nrm[2]), even_w_in[e], fox_b_f[e], diff_lambda[e],
                                         diff_subln_g[e], even_w_out[e], lam_init)
            ms, b1, b2, b3 = even_sample(rmsnorm(ys, nrm[2]), e, cache_diff_kv, cache_fox_kv, cache_fox_logf,
                                         page_table, even_w_in[e], fox_b_f[e], diff_lambda[e],
                                         diff_subln_g[e], even_w_out[e], lam_init)
            dkv_p.append(a1); fkv_p.append(a2); fl_p.append(a3)
            dkv_s.append(b1); fkv_s.append(b2); fl_s.append(b3)
        else:
            o = i // 2
            mp, a1, a2, a3 = odd_prompt(rmsnorm(yp, nrm[2]), odd_w_in[o], gmlp_ln_g[o], gmlp_ln_b[o],
                                        gmlp_w_s[o], gmlp_b_s[o], odd_w_out[o])
            ms, b1, b2, b3, b4 = odd_sample(rmsnorm(ys, nrm[2]), o, cache_nsa_cmp_kv, cache_nsa_sel_kv,
                                            state_nsa_win_kv, page_table, odd_w_in[o], gmlp_ln_g[o],
                                            gmlp_ln_b[o], gmlp_w_s[o], gmlp_b_s[o], odd_w_out[o])
            ckv_p.append(a1); skv_p.append(a2); wkv_p.append(a3)
            ckv_s.append(b1); skv_s.append(b2); wkv_s.append(b3); gv_s.append(b4)
        yp = yp + rmsnorm(mp, nrm[3])
        ys = ys + rmsnorm(ms, nrm[3])
        yp = ffn_half(yp, nrm[4], nrm[5], ffn_w_in[i, 1], ffn_w_out[i, 1])
        ys = ffn_half(ys, nrm[4], nrm[5], ffn_w_in[i, 1], ffn_w_out[i, 1])
    y_prompt = yp
    y_sample = ys
    new_diff_kv_prompt = jnp.stack(dkv_p, axis=0)
    new_diff_kv_sample = jnp.stack(dkv_s, axis=0)
    new_fox_kv_prompt = jnp.stack(fkv_p, axis=0)
    new_fox_kv_sample = jnp.stack(fkv_s, axis=0)
    new_fox_logf_prompt = jnp.stack(fl_p, axis=0)
    new_fox_logf_sample = jnp.stack(fl_s, axis=0)
    new_nsa_cmp_kv_prompt = jnp.stack(ckv_p, axis=0)
    new_nsa_cmp_kv_sample = jnp.stack(ckv_s, axis=0)
    new_nsa_sel_kv_prompt = jnp.stack(skv_p, axis=0)
    new_nsa_sel_kv_sample = jnp.stack(skv_s, axis=0)
    new_nsa_win_kv_prompt = jnp.stack(wkv_p, axis=0)
    new_nsa_win_kv_sample = jnp.stack(wkv_s, axis=0)
    new_gmlp_v_sample = jnp.stack(gv_s, axis=0)
    return (y_prompt, y_sample, new_diff_kv_prompt, new_diff_kv_sample, new_fox_kv_prompt, new_fox_kv_sample,
            new_fox_logf_prompt, new_fox_logf_sample, new_nsa_cmp_kv_prompt, new_nsa_cmp_kv_sample,
            new_nsa_sel_kv_prompt, new_nsa_sel_kv_sample, new_nsa_win_kv_prompt, new_nsa_win_kv_sample,
            new_gmlp_v_sample)
```

```python
import functools
import math

import jax
import jax.numpy as jnp
from jax import lax
from jax.experimental import pallas as pl
from jax.experimental.pallas import tpu as pltpu

F32 = jnp.float32
BF16 = jnp.bfloat16

D_MODEL = 1024
DEPTH = 4
HD = 64
ROT_DIM = HD // 4
ROPE_THETA = 500000.0
A_HEADS = 4
B_HEADS = 8
C_CH = 512
C_GROUPS = 4
C_GW = C_CH // C_GROUPS
CHUNK = 128
D_HEADS = 8
D_KV = 2
D_REP = D_HEADS // D_KV
CMP_BLOCK = 32
SEL_BLOCK = 64
N_SEL = 8
WINDOW = 512
D_FF = 2816
EPS = 1e-6
NEG = -1e30
TINY = 1e-30
PAGE_SIZE = 128
PAST_LEN = 2048
N_PAGES = PAST_LEN // PAGE_SIZE
N_PAST_SEL = PAST_LEN // SEL_BLOCK
N_PAST_CMP = PAST_LEN // CMP_BLOCK
QSCALE = HD ** -0.5

LANES = 128
HALF = LANES // 2
VMEM_LIMIT = 56 * 2 ** 20
TOKEN_TILE = 256
FFN_CHUNK = D_FF // 2
EVEN_COLS_PAD = 3200
ODD_COLS_PAD = 2432
Q_TILE = 256
NSA_Q = 128
NSA_TK = 512
NT_DIMS = (((1,), (1,)), ((), ()))
TN_DIMS = (((0,), (0,)), ((), ()))
HIGHEST = lax.Precision.HIGHEST


def _cparams(sem=None):
    return pltpu.CompilerParams(dimension_semantics=sem, vmem_limit_bytes=VMEM_LIMIT)


def _resident(shape):
    nd = len(shape)
    return pl.BlockSpec(shape, lambda *_: (0,) * nd, pipeline_mode=pl.Buffered(1))


def _rms(x, g):
    return x * lax.rsqrt(jnp.mean(x * x, axis=-1, keepdims=True) + EPS) * g


def _lane_tile(t, width):
    return jnp.concatenate([t] * (width // LANES), axis=1)


def _rope(x, c, s1, s2):
    w = x.shape[1]
    return (x * _lane_tile(c, w) + pltpu.roll(x, w - ROT_DIM // 2, 1) * _lane_tile(s1, w)
            + pltpu.roll(x, ROT_DIM // 2, 1) * _lane_tile(s2, w))


def _dup_halves(x):
    r = pltpu.roll(x, HALF, 1)
    lo = lax.broadcasted_iota(jnp.int32, x.shape, 1) < HALF
    return jnp.where(lo, x, r), jnp.where(lo, r, x)


def _ffn_half(x, g_pre, g_post, win_ref, wout_ref):
    xn = _rms(x, g_pre).astype(BF16)
    acc = None
    for c in range(D_FF // FFN_CHUNK):
        lo = c * FFN_CHUNK
        gate = jnp.dot(xn, win_ref[:, lo:lo + FFN_CHUNK], preferred_element_type=F32)
        up = jnp.dot(xn, win_ref[:, D_FF + lo:D_FF + lo + FFN_CHUNK], preferred_element_type=F32)
        act = (jax.nn.silu(gate) * up).astype(BF16)
        part = jnp.dot(act, wout_ref[lo:lo + FFN_CHUNK, :], preferred_element_type=F32)
        acc = part if acc is None else acc + part
    return x + 0.5 * _rms(acc, g_post)


def _cumsum_lanes(x):
    n = x.shape[1]
    lane = lax.broadcasted_iota(jnp.int32, x.shape, 1)
    s = 1
    while s < n:
        x = x + jnp.where(lane >= s, pltpu.roll(x, s, 1), 0.0)
        s *= 2
    return x


def _topk_rows(score, k):
    n = score.shape[0]
    row = lax.broadcasted_iota(jnp.int32, score.shape, 0).astype(F32)
    sel = jnp.zeros(score.shape, F32)
    for _ in range(k):
        m = jnp.max(score, axis=0, keepdims=True)
        first = jnp.min(jnp.where(score == m, row, float(n)), axis=0, keepdims=True)
        pick = row == first
        sel = jnp.where(pick, 1.0, sel)
        score = jnp.where(pick, -2.0, score)
    return sel


def _masked_softmax_rows(s, mask):
    s = jnp.where(mask, s, NEG)
    m = jnp.max(s, axis=-1, keepdims=True)
    p = jnp.where(mask, jnp.exp(s - m), 0.0)
    return p / jnp.maximum(jnp.sum(p, axis=-1, keepdims=True), TINY)


def _even_pre_body(x_ref, nrm_ref, win_ref, wout_ref, wp_ref, bf_ref, c_ref, s1_ref, s2_ref,
                   h_ref, q_ref, dkv_ref, fkv_ref, k_ref, v_ref, logf_ref):
    h = _ffn_half(x_ref[...], nrm_ref[0:1, :], nrm_ref[1:2, :], win_ref, wout_ref)
    h_ref[...] = h
    hn = _rms(h, nrm_ref[2:3, :]).astype(BF16)
    z = jnp.dot(hn, wp_ref[...], preferred_element_type=F32)
    c, s1, s2 = c_ref[...], s1_ref[...], s2_ref[...]
    w = A_HEADS * 2 * HD
    aq = _rope(z[:, 0:w], c, s1, s2)
    bq = z[:, w:2 * w]
    ak = _rope(z[:, 2 * w:3 * w], c, s1, s2)
    av = z[:, 3 * w:4 * w]
    bk = z[:, 4 * w:5 * w]
    bv = z[:, 5 * w:6 * w]
    q_ref[:, 0:w] = (aq * QSCALE).astype(BF16)
    q_ref[:, w:2 * w] = (bq * QSCALE).astype(BF16)
    dkv_ref[:, 0:w] = ak
    dkv_ref[:, w:2 * w] = av
    fkv_ref[:, 0:w] = bk
    fkv_ref[:, w:2 * w] = bv
    k_ref[:, 0:w] = ak.astype(BF16)
    k_ref[:, w:2 * w] = bk.astype(BF16)
    v_ref[:, 0:w] = av.astype(BF16)
    v_ref[:, w:2 * w] = bv.astype(BF16)
    lf = jax.nn.log_sigmoid(z[:, 6 * w:6 * w + LANES] + bf_ref[...])
    logf_ref[...] = lf[:, 0:B_HEADS]


def _odd_pre_body(x_ref, nrm_ref, win_ref, wout_ref, wp_ref, lng_ref, lnb_ref, c_ref, s1_ref, s2_ref,
                  h_ref, u_ref, v_ref, q_ref, cmp_ref, sel_ref, win_kv_ref, nsa_ref, gate_ref):
    h = _ffn_half(x_ref[...], nrm_ref[0:1, :], nrm_ref[1:2, :], win_ref, wout_ref)
    h_ref[...] = h
    hn = _rms(h, nrm_ref[2:3, :]).astype(BF16)
    z = jnp.dot(hn, wp_ref[...], preferred_element_type=F32)
    c, s1, s2 = c_ref[...], s1_ref[...], s2_ref[...]
    u_ref[...] = jax.nn.gelu(z[:, 0:C_CH])
    gv = jax.nn.gelu(z[:, C_CH:2 * C_CH])
    for g in range(C_GROUPS):
        blk = gv[:, g * C_GW:(g + 1) * C_GW]
        mu = jnp.mean(blk, axis=-1, keepdims=True)
        var = jnp.mean(jnp.square(blk - mu), axis=-1, keepdims=True)
        v_ref[:, g * C_GW:(g + 1) * C_GW] = ((blk - mu) * lax.rsqrt(var + EPS) * lng_ref[:, g * C_GW:(g + 1) * C_GW]
                                             + lnb_ref[:, g * C_GW:(g + 1) * C_GW])
    o = 2 * C_CH
    wq = D_HEADS * HD
    q = z[:, o:o + wq]
    q_ref[:, 0:wq] = (q * QSCALE).astype(BF16)
    q_ref[:, wq:2 * wq] = (_rope(q, c, s1, s2) * QSCALE).astype(BF16)
    o += wq
    kvw = D_KV * HD
    ck, cv = z[:, o:o + kvw], z[:, o + kvw:o + 2 * kvw]
    sk, sv = _rope(z[:, o + 2 * kvw:o + 3 * kvw], c, s1, s2), z[:, o + 3 * kvw:o + 4 * kvw]
    wk, wv = _rope(z[:, o + 4 * kvw:o + 5 * kvw], c, s1, s2), z[:, o + 5 * kvw:o + 6 * kvw]
    cmp_ref[:, 0:kvw] = ck
    cmp_ref[:, kvw:2 * kvw] = cv
    sel_ref[:, 0:kvw] = sk
    sel_ref[:, kvw:2 * kvw] = sv
    win_kv_ref[:, 0:kvw] = wk
    win_kv_ref[:, kvw:2 * kvw] = wv
    for j, arr in enumerate((sk, sv, wk, wv)):
        d0, d1 = _dup_halves(arr)
        nsa_ref[:, (2 * j) * LANES:(2 * j + 1) * LANES] = d0.astype(BF16)
        nsa_ref[:, (2 * j + 1) * LANES:(2 * j + 2) * LANES] = d1.astype(BF16)
    o += 6 * kvw
    gate_ref[...] = jax.nn.sigmoid(z[:, o:o + LANES])[:, 0:3 * D_HEADS]


def _mixer_post(o, wmo_ref, h, nrm_ref, win_ref, wout_ref):
    m = jnp.dot(o.astype(BF16), wmo_ref[...], preferred_element_type=F32)
    h2 = h + _rms(m, nrm_ref[3:4, :])
    return _ffn_half(h2, nrm_ref[4:5, :], nrm_ref[5:6, :], win_ref, wout_ref)


def _even_post_body(oa_ref, ob_ref, sg_ref, wmo_ref, h_ref, nrm_ref, win_ref, wout_ref, y_ref, *, out_scale):
    sg = sg_ref[...]
    parts = []
    for hd in range(A_HEADS):
        blk = oa_ref[:, hd * 2 * HD:(hd + 1) * 2 * HD]
        parts.append(_rms(blk, sg) * out_scale)
    parts.append(ob_ref[...])
    o = jnp.concatenate(parts, axis=1)
    y_ref[...] = _mixer_post(o, wmo_ref, h_ref[...], nrm_ref, win_ref, wout_ref)


def _odd_post_body(u_ref, v_ref, od_ref, ws_ref, bs_ref, wmo_ref, h_ref, nrm_ref, win_ref, wout_ref, y_ref,
                   *, period):
    tm = u_ref.shape[0]
    r = lax.broadcasted_iota(jnp.int32, (tm, tm), 0)
    cidx = lax.broadcasted_iota(jnp.int32, (tm, tm), 1)
    keep = (r // period == cidx // period) & (cidx % period <= r % period)
    parts = []
    for g in range(C_GROUPS):
        wm = jnp.where(keep, ws_ref[g], 0.0).astype(BF16)
        vg = v_ref[:, g * C_GW:(g + 1) * C_GW].astype(BF16)
        mix = jnp.dot(wm, vg, preferred_element_type=F32) + bs_ref[:, g * C_GW:(g + 1) * C_GW]
        parts.append(u_ref[:, g * C_GW:(g + 1) * C_GW] * mix)
    parts.append(od_ref[...])
    o = jnp.concatenate(parts, axis=1)
    y_ref[...] = _mixer_post(o, wmo_ref, h_ref[...], nrm_ref, win_ref, wout_ref)


def _row_spec(tm, width):
    return pl.BlockSpec((tm, width), lambda i: (i, 0))


def _table_spec(tm, rows):
    nrep = rows // tm
    return pl.BlockSpec((tm, LANES), lambda i: (i % nrep, 0))


def _even_pre(x, nrm, win, wout, wp, bf_pad, tables):
    t = x.shape[0]
    tm = TOKEN_TILE
    tabs = [_table_spec(tm, tables[0].shape[0])] * 3
    outs = (jax.ShapeDtypeStruct((t, D_MODEL), F32), jax.ShapeDtypeStruct((t, 1024), BF16),
            jax.ShapeDtypeStruct((t, 1024), F32), jax.ShapeDtypeStruct((t, 1024), F32),
            jax.ShapeDtypeStruct((t, 1024), BF16), jax.ShapeDtypeStruct((t, 1024), BF16),
            jax.ShapeDtypeStruct((t, B_HEADS), F32))
    return pl.pallas_call(
        _even_pre_body, out_shape=outs, grid=(t // tm,),
        in_specs=[_row_spec(tm, D_MODEL), _resident(nrm.shape), _resident(win.shape), _resident(wout.shape),
                  _resident(wp.shape), _resident(bf_pad.shape)] + tabs,
        out_specs=(_row_spec(tm, D_MODEL), _row_spec(tm, 1024), _row_spec(tm, 1024), _row_spec(tm, 1024),
                   _row_spec(tm, 1024), _row_spec(tm, 1024), _row_spec(tm, B_HEADS)),
        compiler_params=_cparams(("arbitrary",)), name="even_pre")(x, nrm, win, wout, wp, bf_pad, *tables)


def _odd_pre(x, nrm, win, wout, wp, lng, lnb, tables):
    t = x.shape[0]
    tm = TOKEN_TILE
    tabs = [_table_spec(tm, tables[0].shape[0])] * 3
    kv2 = 2 * D_KV * HD
    outs = (jax.ShapeDtypeStruct((t, D_MODEL), F32), jax.ShapeDtypeStruct((t, C_CH), F32),
            jax.ShapeDtypeStruct((t, C_CH), F32), jax.ShapeDtypeStruct((t, 1024), BF16),
            jax.ShapeDtypeStruct((t, kv2), F32), jax.ShapeDtypeStruct((t, kv2), F32),
            jax.ShapeDtypeStruct((t, kv2), F32), jax.ShapeDtypeStruct((t, 1024), BF16),
            jax.ShapeDtypeStruct((t, 3 * D_HEADS), F32))
    return pl.pallas_call(
        _odd_pre_body, out_shape=outs, grid=(t // tm,),
        in_specs=[_row_spec(tm, D_MODEL), _resident(nrm.shape), _resident(win.shape), _resident(wout.shape),
                  _resident(wp.shape), _resident(lng.shape), _resident(lnb.shape)] + tabs,
        out_specs=(_row_spec(tm, D_MODEL), _row_spec(tm, C_CH), _row_spec(tm, C_CH), _row_spec(tm, 1024),
                   _row_spec(tm, kv2), _row_spec(tm, kv2), _row_spec(tm, kv2), _row_spec(tm, 1024),
                   _row_spec(tm, 3 * D_HEADS)),
        compiler_params=_cparams(("arbitrary",)), name="odd_pre")(x, nrm, win, wout, wp, lng, lnb, *tables)


def _even_post(oa, ob, sg, wmo, h, nrm, win, wout, out_scale):
    t = h.shape[0]
    tm = TOKEN_TILE
    return pl.pallas_call(
        functools.partial(_even_post_body, out_scale=out_scale),
        out_shape=jax.ShapeDtypeStruct((t, D_MODEL), F32), grid=(t // tm,),
        in_specs=[_row_spec(tm, 512), _row_spec(tm, 512), _resident(sg.shape), _resident(wmo.shape),
                  _row_spec(tm, D_MODEL), _resident(nrm.shape), _resident(win.shape), _resident(wout.shape)],
        out_specs=_row_spec(tm, D_MODEL),
        compiler_params=_cparams(("arbitrary",)), name="even_post")(oa, ob, sg, wmo, h, nrm, win, wout)


def _odd_post(u, v, od, ws_tiled, bs_full, wmo, h, nrm, win, wout, period):
    t = h.shape[0]
    tm = TOKEN_TILE
    return pl.pallas_call(
        functools.partial(_odd_post_body, period=period),
        out_shape=jax.ShapeDtypeStruct((t, D_MODEL), F32), grid=(t // tm,),
        in_specs=[_row_spec(tm, C_CH), _row_spec(tm, C_CH), _row_spec(tm, 512), _resident(ws_tiled.shape),
                  _resident(bs_full.shape), _resident(wmo.shape), _row_spec(tm, D_MODEL), _resident(nrm.shape),
                  _resident(win.shape), _resident(wout.shape)],
        out_specs=_row_spec(tm, D_MODEL),
        compiler_params=_cparams(("arbitrary",)), name="odd_post")(u, v, od, ws_tiled, bs_full, wmo, h, nrm, win,
                                                                   wout)


def _lambda_value(lam_ref, lam_init):
    lp = lam_ref[...]
    a = jnp.sum(lp[0:1, :] * lp[1:2, :], axis=-1, keepdims=True)
    b = jnp.sum(lp[2:3, :] * lp[3:4, :], axis=-1, keepdims=True)
    return jnp.exp(a) - jnp.exp(b) + lam_init


def _cumsum_body(x_ref, o_ref):
    o_ref[...] = _cumsum_lanes(x_ref[...])


def _cumsum_rows_of_lanes(x):
    return pl.pallas_call(_cumsum_body, out_shape=jax.ShapeDtypeStruct(x.shape, F32), name="fox_cumsum",
                          compiler_params=_cparams())(x)


def _pair_attn_body(*refs, fox, lam_init):
    if fox:
        q_ref, k_ref, v_ref, fcol_ref, frow_ref, o_ref, m_sc, l_sc, acc_sc = refs
    else:
        q_ref, k_ref, v_ref, lam_ref, o_ref, m_sc, l_sc, acc_sc = refs
    tq = q_ref.shape[0]
    tk = tq
    qi = pl.program_id(2)
    q = q_ref[...]
    lo = lax.broadcasted_iota(jnp.int32, q.shape, 1) < HALF
    zero = jnp.zeros_like(q)
    qs = (jnp.where(lo, q, zero), jnp.where(lo, zero, q))
    m_sc[...] = jnp.full(m_sc.shape, NEG, F32)
    l_sc[...] = jnp.zeros(l_sc.shape, F32)
    acc_sc[...] = jnp.zeros(acc_sc.shape, F32)
    qpos = qi * tq + lax.broadcasted_iota(jnp.int32, (tq, tk), 0)
    koff = lax.broadcasted_iota(jnp.int32, (tq, tk), 1)

    def step(kb, carry):
        ks = pl.multiple_of(kb * tk, tk)
        k = k_ref[pl.ds(ks, tk), :]
        v = v_ref[pl.ds(ks, tk), :]
        mask = (ks + koff) <= qpos
        for c in range(2):
            s = lax.dot_general(qs[c], k, NT_DIMS, preferred_element_type=F32)
            if fox:
                s = s + fcol_ref[:, c:c + 1] - frow_ref[c:c + 1, pl.ds(ks, tk)]
            s = jnp.where(mask, s, NEG)
            m_old = m_sc[c]
            m_new = jnp.maximum(m_old, jnp.max(s, axis=-1, keepdims=True))
            alpha = jnp.exp(m_old - m_new)
            p = jnp.where(mask, jnp.exp(s - m_new), 0.0)
            l_sc[c] = alpha * l_sc[c] + jnp.sum(p, axis=-1, keepdims=True)
            acc_sc[c] = alpha * acc_sc[c] + jnp.dot(p.astype(BF16), v, preferred_element_type=F32)
            m_sc[c] = m_new
        return carry

    lax.fori_loop(0, qi + 1, step, 0)
    o0 = acc_sc[0] / jnp.maximum(l_sc[0], TINY)
    o1 = acc_sc[1] / jnp.maximum(l_sc[1], TINY)
    if fox:
        o_ref[...] = jnp.where(lo, o0, o1)
    else:
        o_ref[...] = o0 - _lambda_value(lam_ref, lam_init) * o1


def _pair_attn_prompt(q, k, v, extra, *, fox, lam_init=0.0):
    b, s, _ = q.shape
    tq = Q_TILE
    off = 4 if fox else 0
    in_specs = [pl.BlockSpec((None, tq, LANES), lambda bi, j, qi: (bi, qi, j + off)),
                pl.BlockSpec((None, s, LANES), lambda bi, j, qi: (bi, 0, j + off)),
                pl.BlockSpec((None, s, LANES), lambda bi, j, qi: (bi, 0, j + off))]
    if fox:
        fcol, frow = extra
        in_specs += [pl.BlockSpec((None, None, tq, 2), lambda bi, j, qi: (bi, j, qi, 0)),
                     pl.BlockSpec((None, None, 2, s), lambda bi, j, qi: (bi, j, 0, 0))]
        args = (q, k, v, fcol, frow)
    else:
        in_specs += [pl.BlockSpec(extra.shape, lambda bi, j, qi: (0, 0))]
        args = (q, k, v, extra)
    return pl.pallas_call(
        functools.partial(_pair_attn_body, fox=fox, lam_init=lam_init),
        out_shape=jax.ShapeDtypeStruct((b, s, 512), F32), grid=(b, 4, s // tq),
        in_specs=in_specs,
        out_specs=pl.BlockSpec((None, tq, LANES), lambda bi, j, qi: (bi, qi, j)),
        scratch_shapes=[pltpu.VMEM((2, tq, 1), F32), pltpu.VMEM((2, tq, 1), F32), pltpu.VMEM((2, tq, LANES), F32)],
        compiler_params=_cparams(("arbitrary", "arbitrary", "arbitrary")),
        name="fox_prompt" if fox else "diff_prompt")(*args)


def _paged_attend(q_ref, pages, score_page, pv_page, new_ref, bias_past, bias_new, t_row):
    w = q_ref.shape[1]
    qb = q_ref[...]
    s = jnp.concatenate([score_page(qb, pg) for pg in pages], axis=1)
    if bias_past is not None:
        s = s + bias_past
    qf = qb.astype(F32)
    nt = new_ref.shape[0]
    s_new = []
    for j in range(nt):
        sj = jnp.sum(qf * new_ref[j:j + 1, 0:w], axis=-1, keepdims=True)
        if bias_new is not None:
            sj = sj + bias_new[j]
        s_new.append(jnp.where(t_row >= j, sj, NEG))
    m = jnp.max(s, axis=-1, keepdims=True)
    for sj in s_new:
        m = jnp.maximum(m, sj)
    p = jnp.exp(s - m)
    p_new = [jnp.where(t_row >= j, jnp.exp(s_new[j] - m), 0.0) for j in range(nt)]
    l = jnp.sum(p, axis=-1, keepdims=True)
    for pj in p_new:
        l = l + pj
    pb = p.astype(BF16)
    acc = None
    for i, pg in enumerate(pages):
        part = pv_page(pb[:, i * PAGE_SIZE:(i + 1) * PAGE_SIZE], pg)
        acc = part if acc is None else acc + part
    for j in range(nt):
        acc = acc + p_new[j] * new_ref[j:j + 1, w:2 * w]
    return acc / jnp.maximum(l, TINY)


def _diff_page_slab(pg, first):
    return jnp.concatenate([pg[pl.ds(first + hd, PAGE_SIZE, stride=2 * A_HEADS), :] for hd in range(A_HEADS)],
                           axis=1).astype(BF16)


def _diff_score_page(q, pg):
    return lax.dot_general(q, _diff_page_slab(pg, 0), NT_DIMS, preferred_element_type=F32)


def _diff_pv_page(p, pg):
    return jnp.dot(p, _diff_page_slab(pg, A_HEADS), preferred_element_type=F32)


def _fox_score_page(q, pg):
    return jnp.dot(q, pg[0:B_HEADS * HD, :].astype(BF16), preferred_element_type=F32)


def _fox_pv_page(p, pg):
    return lax.dot_general(p, pg[B_HEADS * HD:2 * B_HEADS * HD, :].astype(BF16), NT_DIMS,
                           preferred_element_type=F32)


def _even_sample_body(pt_ref, qa_ref, qb_ref, dnew_ref, fnew_ref, lnew_ref, lam_ref, *rest, lam_init):
    del pt_ref
    dpages = rest[0:N_PAGES]
    fpages = rest[N_PAGES:2 * N_PAGES]
    lpages = rest[2 * N_PAGES:3 * N_PAGES]
    o_ref = rest[3 * N_PAGES]
    nt = dnew_ref.shape[0]
    nrow = qa_ref.shape[0]
    w = qa_ref.shape[1]
    rid = lax.broadcasted_iota(jnp.int32, (nrow, 1), 0)
    t_row = rid // 8
    h_row = rid % 8
    lane = lax.broadcasted_iota(jnp.int32, (nrow, w), 1)

    res = _paged_attend(qa_ref, dpages, _diff_score_page, _diff_pv_page, dnew_ref, None, None, t_row)
    lam = _lambda_value(lam_ref, lam_init)
    coef = jnp.where(h_row % 2 == 0, 1.0, -lam)
    wgt = jnp.where(lane // (2 * HD) == h_row // 2, coef, 0.0)
    o_ref[:, 0:w] = jnp.sum((res * wgt).reshape(nt, 8, w), axis=1)

    lt = jnp.concatenate([pg[...] for pg in lpages], axis=1)
    cum = _cumsum_lanes(lt)
    tot = jnp.concatenate([jnp.sum(lt, axis=1, keepdims=True)] * nt, axis=0)
    lnew = lnew_ref[...]
    tix = lax.broadcasted_iota(jnp.int32, lnew.shape, 0)
    cnew = jnp.zeros_like(lnew)
    for j in range(nt):
        cnew = cnew + jnp.where(tix >= j, lnew[j:j + 1, :], 0.0)
    hsel = lax.broadcasted_iota(jnp.int32, (nrow, B_HEADS), 1) == h_row
    cnew_rows = jnp.broadcast_to(cnew[:, None, :], (nt, 8, B_HEADS)).reshape(nrow, B_HEADS)
    fq_new = jnp.sum(jnp.where(hsel, cnew_rows, 0.0), axis=-1, keepdims=True)
    bias_past = (tot + fq_new) - jnp.concatenate([cum] * nt, axis=0)
    bias_new = [fq_new - jnp.sum(jnp.where(hsel, cnew[j:j + 1, :], 0.0), axis=-1, keepdims=True)
                for j in range(nt)]
    res = _paged_attend(qb_ref, fpages, _fox_score_page, _fox_pv_page, fnew_ref, bias_past, bias_new, t_row)
    wgt = jnp.where(lane // HD == h_row, 1.0, 0.0)
    o_ref[:, w:2 * w] = jnp.sum((res * wgt).reshape(nt, 8, w), axis=1)


def _even_sample_attn(e, page_table, qbd_a, qbd_b, dkv_new, fkv_new, lnew, lam_p, cache_d, cache_f, cache_l,
                      lam_init):
    nb, nt = dkv_new.shape[0], dkv_new.shape[1]

    def per_b(shape):
        nd = len(shape)
        return pl.BlockSpec((None,) + tuple(shape[1:]), lambda b, pt: (b,) + (0,) * (nd - 1))

    def page(rows, p):
        return pl.BlockSpec((None, None, rows, PAGE_SIZE), lambda b, pt: (e, pt[b, p], 0, 0))

    in_specs = [per_b(qbd_a.shape), per_b(qbd_b.shape), per_b(dkv_new.shape), per_b(fkv_new.shape),
                per_b(lnew.shape), pl.BlockSpec(lam_p.shape, lambda b, pt: (0, 0))]
    in_specs += [page(cache_d.shape[2], p) for p in range(N_PAGES)]
    in_specs += [page(cache_f.shape[2], p) for p in range(N_PAGES)]
    in_specs += [page(B_HEADS, p) for p in range(N_PAGES)]
    gs = pltpu.PrefetchScalarGridSpec(
        num_scalar_prefetch=1, grid=(nb,), in_specs=in_specs,
        out_specs=pl.BlockSpec((None, nt, 1024), lambda b, pt: (b, 0, 0)))
    return pl.pallas_call(
        functools.partial(_even_sample_body, lam_init=lam_init),
        out_shape=jax.ShapeDtypeStruct((nb, nt, 1024), F32), grid_spec=gs,
        compiler_params=_cparams(("arbitrary",)), name="even_sample")(
            page_table, qbd_a, qbd_b, dkv_new, fkv_new, lnew, lam_p,
            *([cache_d] * N_PAGES), *([cache_f] * N_PAGES), *([cache_l] * N_PAGES))


def _cmp_means_body(x_ref, o_ref):
    x = x_ref[...]
    n = x.shape[0] // CMP_BLOCK
    m = jnp.sum(x.reshape(n, CMP_BLOCK, x.shape[1]), axis=1) * (1.0 / CMP_BLOCK)
    for j in range(2):
        d0, d1 = _dup_halves(m[:, j * LANES:(j + 1) * LANES])
        o_ref[:, (2 * j) * LANES:(2 * j + 1) * LANES] = d0.astype(BF16)
        o_ref[:, (2 * j + 1) * LANES:(2 * j + 2) * LANES] = d1.astype(BF16)


def _cmp_means(cmp_kv):
    b, s, w = cmp_kv.shape
    return pl.pallas_call(
        _cmp_means_body, out_shape=jax.ShapeDtypeStruct((b, s // CMP_BLOCK, 2 * w), BF16), grid=(b,),
        in_specs=[pl.BlockSpec((None, s, w), lambda i: (i, 0, 0))],
        out_specs=pl.BlockSpec((None, s // CMP_BLOCK, 2 * w), lambda i: (i, 0, 0)),
        compiler_params=_cparams(("arbitrary",)), name="nsa_cmp_means")(cmp_kv)


def _nsa_prompt_body(q_ref, kc_ref, kv_ref, gate_ref, o_ref, m_sc, l_sc, acc_sc):
    nq = q_ref.shape[0]
    s_len = kv_ref.shape[0]
    n_cmp = kc_ref.shape[0]
    n_sel = s_len // SEL_BLOCK
    qi = pl.program_id(1)
    q0 = qi * nq
    nrow = D_REP * nq
    lo_q = lax.broadcasted_iota(jnp.int32, (nq, LANES), 1) < HALF
    qpos_col = q0 + lax.broadcasted_iota(jnp.int32, (nrow, 1), 0) % nq
    gates = gate_ref[...]
    wq = D_HEADS * HD

    def stack_q(base, g):
        parts = []
        for rep in range(D_REP):
            hd = g * D_REP + rep
            slab = q_ref[:, base + (hd // 2) * LANES:base + (hd // 2 + 1) * LANES]
            keep = lo_q if hd % 2 == 0 else jnp.logical_not(lo_q)
            parts.append(jnp.where(keep, slab, jnp.zeros_like(slab)))
        return jnp.concatenate(parts, axis=0)

    heads = [None] * D_HEADS
    for g in range(D_KV):
        qc = stack_q(0, g)
        qr = stack_q(wq, g)
        kc = kc_ref[:, g * LANES:(g + 1) * LANES]
        vc = kc_ref[:, (2 + g) * LANES:(3 + g) * LANES]
        sc = lax.dot_general(qc, kc, NT_DIMS, preferred_element_type=F32)
        c_end = lax.broadcasted_iota(jnp.int32, (nrow, n_cmp), 1) * CMP_BLOCK + (CMP_BLOCK - 1)
        pc = _masked_softmax_rows(sc, c_end <= qpos_col)
        o_c = jnp.dot(pc.astype(BF16), vc, preferred_element_type=F32)
        imp_c = pc[0:nq] + pc[nq:2 * nq] + pc[2 * nq:3 * nq] + pc[3 * nq:4 * nq]
        pr = lax.broadcasted_iota(jnp.int32, (n_sel, n_cmp), 0)
        pcx = lax.broadcasted_iota(jnp.int32, (n_sel, n_cmp), 1)
        pair_t = jnp.where(pcx // (SEL_BLOCK // CMP_BLOCK) == pr, 1.0, 0.0)
        imp_t = lax.dot_general(pair_t, imp_c, NT_DIMS, precision=HIGHEST, preferred_element_type=F32)
        blk = lax.broadcasted_iota(jnp.int32, (n_sel, nq), 0)
        qp = q0 + lax.broadcasted_iota(jnp.int32, (n_sel, nq), 1)
        score = jnp.where(qp // SEL_BLOCK == blk, D_REP + 1.0, jnp.where(blk * SEL_BLOCK <= qp, imp_t, -1.0))
        sel_t = _topk_rows(score, min(N_SEL, n_sel)).astype(BF16)

        m_sc[...] = jnp.full(m_sc.shape, NEG, F32)
        l_sc[...] = jnp.zeros(l_sc.shape, F32)
        acc_sc[...] = jnp.zeros(acc_sc.shape, F32)
        tk = NSA_TK
        eb_r = lax.broadcasted_iota(jnp.int32, (n_sel, tk), 0)
        eb_c = lax.broadcasted_iota(jnp.int32, (n_sel, tk), 1)
        koff = lax.broadcasted_iota(jnp.int32, (nrow, tk), 1)

        def step(kb, carry):
            ks = pl.multiple_of(kb * tk, tk)
            k = kv_ref[pl.ds(ks, tk), g * LANES:(g + 1) * LANES]
            v = kv_ref[pl.ds(ks, tk), (2 + g) * LANES:(3 + g) * LANES]
            eb = jnp.where((ks + eb_c) // SEL_BLOCK == eb_r, 1.0, 0.0).astype(BF16)
            msel = lax.dot_general(sel_t, eb, TN_DIMS, preferred_element_type=F32)
            msel = jnp.concatenate([msel] * D_REP, axis=0)
            mask = (msel > 0.5) & ((ks + koff) <= qpos_col)
            s = lax.dot_general(qr, k, NT_DIMS, preferred_element_type=F32)
            s = jnp.where(mask, s, NEG)
            m_old = m_sc[...]
            m_new = jnp.maximum(m_old, jnp.max(s, axis=-1, keepdims=True))
            alpha = jnp.exp(m_old - m_new)
            p = jnp.where(mask, jnp.exp(s - m_new), 0.0)
            l_sc[...] = alpha * l_sc[...] + jnp.sum(p, axis=-1, keepdims=True)
            acc_sc[...] = alpha * acc_sc[...] + jnp.dot(p.astype(BF16), v, preferred_element_type=F32)
            m_sc[...] = m_new
            return carry

        lax.fori_loop(0, (q0 + nq + tk - 1) // tk, step, 0)
        o_s = acc_sc[...] / jnp.maximum(l_sc[...], TINY)

        wlen = WINDOW + nq
        ws = pl.multiple_of(jnp.maximum(q0 - WINDOW, 0), nq)
        wk = kv_ref[pl.ds(ws, wlen), (4 + g) * LANES:(5 + g) * LANES]
        wv = kv_ref[pl.ds(ws, wlen), (6 + g) * LANES:(7 + g) * LANES]
        wpos = ws + lax.broadcasted_iota(jnp.int32, (nrow, wlen), 1)
        wmask = (wpos <= qpos_col) & (wpos > qpos_col - WINDOW)
        pw = _masked_softmax_rows(lax.dot_general(qr, wk, NT_DIMS, preferred_element_type=F32), wmask)
        o_w = jnp.dot(pw.astype(BF16), wv, preferred_element_type=F32)

        for rep in range(D_REP):
            hd = g * D_REP + rep
            rows = slice(rep * nq, (rep + 1) * nq)
            heads[hd] = (gates[:, 3 * hd:3 * hd + 1] * o_c[rows] + gates[:, 3 * hd + 1:3 * hd + 2] * o_s[rows]
                         + gates[:, 3 * hd + 2:3 * hd + 3] * o_w[rows])
    for j in range(D_HEADS // 2):
        o_ref[:, j * LANES:(j + 1) * LANES] = jnp.where(lo_q, heads[2 * j], heads[2 * j + 1])


def _nsa_prompt(q, kcd, nsa_kv, gates):
    b, s, _ = q.shape
    nq = NSA_Q
    return pl.pallas_call(
        _nsa_prompt_body, out_shape=jax.ShapeDtypeStruct((b, s, D_HEADS * HD), F32), grid=(b, s // nq),
        in_specs=[pl.BlockSpec((None, nq, 1024), lambda bi, qi: (bi, qi, 0)),
                  pl.BlockSpec((None,) + kcd.shape[1:], lambda bi, qi: (bi, 0, 0)),
                  pl.BlockSpec((None, s, 1024), lambda bi, qi: (bi, 0, 0)),
                  pl.BlockSpec((None, nq, 3 * D_HEADS), lambda bi, qi: (bi, qi, 0))],
        out_specs=pl.BlockSpec((None, nq, D_HEADS * HD), lambda bi, qi: (bi, qi, 0)),
        scratch_shapes=[pltpu.VMEM((D_REP * nq, 1), F32), pltpu.VMEM((D_REP * nq, 1), F32),
                        pltpu.VMEM((D_REP * nq, LANES), F32)],
        compiler_params=_cparams(("arbitrary", "arbitrary")), name="nsa_prompt")(q, kcd, nsa_kv, gates)


def _nsa_sample_body(pt_ref, qc_ref, qr_ref, gate_ref, cnew_ref, snew_ref, wnew_ref, wst_ref, avg_ref, *rest):
    del pt_ref, cnew_ref
    cpages = rest[0:N_PAGES]
    spages = rest[N_PAGES:2 * N_PAGES]
    o_ref = rest[2 * N_PAGES]
    nrow = qc_ref.shape[0]
    nt = snew_ref.shape[0]
    rid = lax.broadcasted_iota(jnp.int32, (nrow, 1), 0)
    t_row = rid // D_HEADS
    qc = qc_ref[...]
    qr = qr_ref[...]
    qrf = qr.astype(F32)

    cmp_t = jnp.concatenate([pg[...].astype(BF16) for pg in cpages], axis=1)
    means_t = jnp.dot(cmp_t, avg_ref[...], preferred_element_type=F32)
    kc_t = means_t[0:LANES].astype(BF16)
    vc_t = means_t[LANES:2 * LANES].astype(BF16)
    sc = jnp.dot(qc, kc_t, preferred_element_type=F32)
    mc = jnp.max(sc, axis=-1, keepdims=True)
    pc = jnp.exp(sc - mc)
    pc = pc / jnp.maximum(jnp.sum(pc, axis=-1, keepdims=True), TINY)
    o_c = lax.dot_general(pc.astype(BF16), vc_t, NT_DIMS, preferred_element_type=F32)

    ncol = nt * D_KV
    gr = lax.broadcasted_iota(jnp.int32, (ncol, nrow), 0)
    gc = lax.broadcasted_iota(jnp.int32, (ncol, nrow), 1)
    gsum = jnp.where(gc // D_REP == gr, 1.0, 0.0)
    imp_c = jnp.dot(gsum, pc, precision=HIGHEST, preferred_element_type=F32)
    pr = lax.broadcasted_iota(jnp.int32, (N_PAST_SEL, N_PAST_CMP), 0)
    pcx = lax.broadcasted_iota(jnp.int32, (N_PAST_SEL, N_PAST_CMP), 1)
    pair_t = jnp.where(pcx // (SEL_BLOCK // CMP_BLOCK) == pr, 1.0, 0.0)
    imp_t = lax.dot_general(pair_t, imp_c, NT_DIMS, precision=HIGHEST, preferred_element_type=F32)
    sel_t = _topk_rows(imp_t, N_SEL - 1).astype(BF16)
    er = lax.broadcasted_iota(jnp.int32, (nrow, ncol), 0)
    ec = lax.broadcasted_iota(jnp.int32, (nrow, ncol), 1)
    g_t = jnp.where(er // D_REP == ec, 1.0, 0.0).astype(BF16)
    sel_r = lax.dot_general(g_t, sel_t, NT_DIMS, preferred_element_type=F32)
    eb_r = lax.broadcasted_iota(jnp.int32, (N_PAST_SEL, PAST_LEN), 0)
    eb_c = lax.broadcasted_iota(jnp.int32, (N_PAST_SEL, PAST_LEN), 1)
    eb = jnp.where(eb_c // SEL_BLOCK == eb_r, 1.0, 0.0).astype(BF16)
    smask = jnp.dot(sel_r.astype(BF16), eb, preferred_element_type=F32) > 0.5

    def new_rows(s_past, mask_past, new_ref):
        s_past = jnp.where(mask_past, s_past, NEG)
        s_new = [jnp.where(t_row >= j, jnp.sum(qrf * new_ref[j:j + 1, 0:LANES], axis=-1, keepdims=True), NEG)
                 for j in range(nt)]
        m = jnp.max(s_past, axis=-1, keepdims=True)
        for sj in s_new:
            m = jnp.maximum(m, sj)
        p = jnp.where(mask_past, jnp.exp(s_past - m), 0.0)
        p_new = [jnp.where(t_row >= j, jnp.exp(s_new[j] - m), 0.0) for j in range(nt)]
        l = jnp.sum(p, axis=-1, keepdims=True)
        for pj in p_new:
            l = l + pj
        inv = 1.0 / jnp.maximum(l, TINY)
        return p * inv, [pj * inv for pj in p_new]

    s_parts = [jnp.dot(qr, pg[0:LANES, :].astype(BF16), preferred_element_type=F32) for pg in spages]
    ps, ps_new = new_rows(jnp.concatenate(s_parts, axis=1), smask, snew_ref)
    psb = ps.astype(BF16)
    o_s = None
    for i, pg in enumerate(spages):
        part = lax.dot_general(psb[:, i * PAGE_SIZE:(i + 1) * PAGE_SIZE], pg[LANES:2 * LANES, :].astype(BF16),
                               NT_DIMS, preferred_element_type=F32)
        o_s = part if o_s is None else o_s + part
    for j in range(nt):
        o_s = o_s + ps_new[j] * snew_ref[j:j + 1, LANES:2 * LANES]

    wb = wst_ref.shape[1]
    wk_t = wst_ref[0:LANES, :].astype(BF16)
    wv_t = wst_ref[LANES:2 * LANES, :].astype(BF16)
    jpos = lax.broadcasted_iota(jnp.int32, (nrow, wb), 1)
    wmask = jpos > t_row - (WINDOW - wb)
    pw, pw_new = new_rows(jnp.dot(qr, wk_t, preferred_element_type=F32), wmask, wnew_ref)
    o_w = lax.dot_general(pw.astype(BF16), wv_t, NT_DIMS, preferred_element_type=F32)
    for j in range(nt):
        o_w = o_w + pw_new[j] * wnew_ref[j:j + 1, LANES:2 * LANES]

    g = gate_ref[...]
    o_ref[...] = g[:, 0:1] * o_c + g[:, 1:2] * o_s + g[:, 2:3] * o_w


def _nsa_sample_attn(o, page_table, qc_bd, qr_bd, gates, cnew, snew, wnew, win_state, cache_c, cache_s):
    nb = qc_bd.shape[0]
    kv2 = 2 * D_KV * HD
    key = jnp.arange(PAST_LEN)[:, None] // CMP_BLOCK
    avg = jnp.where(key == jnp.arange(N_PAST_CMP)[None, :], 1.0 / CMP_BLOCK, 0.0).astype(BF16)

    def per_b(shape):
        nd = len(shape)
        return pl.BlockSpec((None,) + tuple(shape[1:]), lambda b, pt: (b,) + (0,) * (nd - 1))

    def page(p):
        return pl.BlockSpec((None, None, kv2, PAGE_SIZE), lambda b, pt: (o, pt[b, p], 0, 0))

    in_specs = [per_b(qc_bd.shape), per_b(qr_bd.shape), per_b(gates.shape), per_b(cnew.shape), per_b(snew.shape),
                per_b(wnew.shape),
                pl.BlockSpec((None, None) + win_state.shape[2:], lambda b, pt: (o, b, 0, 0)),
                pl.BlockSpec(avg.shape, lambda b, pt: (0, 0))]
    in_specs += [page(p) for p in range(N_PAGES)]
    in_specs += [page(p) for p in range(N_PAGES)]
    gs = pltpu.PrefetchScalarGridSpec(
        num_scalar_prefetch=1, grid=(nb,), in_specs=in_specs,
        out_specs=pl.BlockSpec((None,) + qc_bd.shape[1:], lambda b, pt: (b, 0, 0)))
    return pl.pallas_call(
        _nsa_sample_body, out_shape=jax.ShapeDtypeStruct(qc_bd.shape, F32), grid_spec=gs,
        compiler_params=_cparams(("arbitrary",)), name="nsa_sample")(
            page_table, qc_bd, qr_bd, gates, cnew, snew, wnew, win_state, avg,
            *([cache_c] * N_PAGES), *([cache_s] * N_PAGES))


def _rope_tables(pos):
    half = ROT_DIM // 2
    inv = 1.0 / (ROPE_THETA ** (jnp.arange(half, dtype=F32) * 2.0 / ROT_DIM))
    ang = pos.astype(F32)[:, None] * inv[None, :]
    cos, sin = jnp.cos(ang), jnp.sin(ang)
    n = pos.shape[0]
    zeros8 = jnp.zeros((n, half), F32)
    rest0 = jnp.zeros((n, HD - ROT_DIM), F32)
    c = jnp.concatenate([cos, cos, jnp.ones((n, HD - ROT_DIM), F32)], axis=1)
    s1 = jnp.concatenate([-sin, zeros8, rest0], axis=1)
    s2 = jnp.concatenate([zeros8, sin, rest0], axis=1)
    return tuple(jnp.concatenate([t, t], axis=1) for t in (c, s1, s2))


def _block_diag_rows(q, group):
    nb, nt, w = q.shape
    keep = (jnp.arange(w)[None, :] // HD) == jnp.arange(8)[:, None]
    return jnp.where(keep[None, None], q[:, :, None, :], jnp.zeros((), q.dtype)).reshape(nb, nt * 8, w)


def _nsa_rows(q):
    nb, nt, _ = q.shape
    qh = q.reshape(nb, nt, D_HEADS, HD)
    z = jnp.zeros_like(qh)
    first = (jnp.arange(D_HEADS) < D_REP)[None, None, :, None]
    out = jnp.concatenate([jnp.where(first, qh, z), jnp.where(first, z, qh)], axis=-1)
    return out.reshape(nb, nt * D_HEADS, 2 * HD)


def kernel(x_prompt, x_sample, cache_diff_kv, cache_fox_kv, cache_fox_logf, cache_nsa_cmp_kv, cache_nsa_sel_kv,
           state_nsa_win_kv, page_table, norm_g, ffn_w_in, ffn_w_out, even_w_in, even_w_out, fox_b_f, diff_lambda,
           diff_subln_g, odd_w_in, odd_w_out, gmlp_ln_g, gmlp_ln_b, gmlp_w_s, gmlp_b_s):
    bsz, seq, _ = x_prompt.shape
    nb, nt, _ = x_sample.shape
    tp = bsz * seq
    ts = nb * nt
    n_pool = cache_diff_kv.shape[1]

    yp = x_prompt.reshape(tp, D_MODEL)
    ys = x_sample.reshape(ts, D_MODEL)
    tab_p = _rope_tables(jnp.arange(seq))
    tab_s = _rope_tables(PAST_LEN + (jnp.arange(ts) % nt))
    win_bf = ffn_w_in.astype(BF16)
    wout_bf = ffn_w_out.astype(BF16)

    kv2 = 2 * D_KV * HD
    keys_last = (0, 1, 3, 4, 5, 2)
    cd = cache_diff_kv.reshape(cache_diff_kv.shape[0], n_pool, PAGE_SIZE * 2 * A_HEADS, 2 * HD)
    cf = cache_fox_kv.transpose(keys_last).reshape(cache_fox_kv.shape[0], n_pool, 2 * B_HEADS * HD, PAGE_SIZE)
    cl = cache_fox_logf.transpose(0, 1, 3, 2)
    cc = cache_nsa_cmp_kv.transpose(keys_last).reshape(cache_nsa_cmp_kv.shape[0], n_pool, kv2, PAGE_SIZE)
    cs = cache_nsa_sel_kv.transpose(keys_last).reshape(cache_nsa_sel_kv.shape[0], n_pool, kv2, PAGE_SIZE)
    wstate = state_nsa_win_kv.transpose(keys_last).reshape(state_nsa_win_kv.shape[0], nb, kv2,
                                                           state_nsa_win_kv.shape[2])

    w512 = 512
    outs = {k: [] for k in ("dkv_p", "dkv_s", "fkv_p", "fkv_s", "fl_p", "fl_s", "ckv_p", "ckv_s", "skv_p", "skv_s",
                            "wkv_p", "wkv_s", "gv_s")}
    for i in range(DEPTH):
        nrm = norm_g[i]
        if i % 2 == 0:
            e = i // 2
            lam_init = 0.8 - 0.6 * math.exp(-0.3 * i)
            w = even_w_in[e]
            aq, ak, av, bq, bk, bv, bfc = jnp.split(w, [512, 1024, 1536, 2048, 2560, 3072], axis=1)
            wp = jnp.concatenate([aq, bq, ak, av, bk, bv, bfc,
                                  jnp.zeros((D_MODEL, EVEN_COLS_PAD - w.shape[1]), F32)], axis=1).astype(BF16)
            bf_pad = jnp.concatenate([fox_b_f[e], jnp.zeros((LANES - B_HEADS,), F32)])[None, :]
            lam_p = diff_lambda[e]
            sg = diff_subln_g[e][None, :]
            wmo = even_w_out[e].astype(BF16)

            hp, q, dkv, fkv, kb, vb, logf = _even_pre(yp, nrm, win_bf[i, 0], wout_bf[i, 0], wp, bf_pad, tab_p)
            q3, k3, v3 = (a.reshape(bsz, seq, 1024) for a in (q, kb, vb))
            oa = _pair_attn_prompt(q3, k3, v3, lam_p, fox=False, lam_init=lam_init)
            logf_t = logf.reshape(bsz, seq, B_HEADS).transpose(0, 2, 1).reshape(bsz * B_HEADS, seq)
            frow = _cumsum_rows_of_lanes(logf_t).reshape(bsz, B_HEADS // 2, 2, seq)
            fcol = frow.transpose(0, 1, 3, 2)
            ob = _pair_attn_prompt(q3, k3, v3, (fcol, frow), fox=True)
            yp = _even_post(oa.reshape(tp, w512), ob.reshape(tp, w512), sg, wmo, hp, nrm, win_bf[i, 1],
                            wout_bf[i, 1], 1.0 - lam_init)
            outs["dkv_p"].append(dkv.reshape(bsz, seq, 2, A_HEADS, 2 * HD))
            outs["fkv_p"].append(fkv.reshape(bsz, seq, 2, B_HEADS, HD))
            outs["fl_p"].append(logf.reshape(bsz, seq, B_HEADS))

            hs, q, dkv, fkv, kb, vb, logf = _even_pre(ys, nrm, win_bf[i, 0], wout_bf[i, 0], wp, bf_pad, tab_s)
            qs3 = q.reshape(nb, nt, 1024)
            o = _even_sample_attn(e, page_table, _block_diag_rows(qs3[:, :, :w512], 8),
                                  _block_diag_rows(qs3[:, :, w512:], 8), dkv.reshape(nb, nt, 1024),
                                  fkv.reshape(nb, nt, 1024), logf.reshape(nb, nt, B_HEADS), lam_p, cd, cf,
                                  cl, lam_init).reshape(ts, 1024)
            ys = _even_post(o[:, :w512], o[:, w512:], sg, wmo, hs, nrm, win_bf[i, 1], wout_bf[i, 1], 1.0 - lam_init)
            outs["dkv_s"].append(dkv.reshape(nb, nt, 2, A_HEADS, 2 * HD))
            outs["fkv_s"].append(fkv.reshape(nb, nt, 2, B_HEADS, HD))
            outs["fl_s"].append(logf.reshape(nb, nt, B_HEADS))
        else:
            o_idx = i // 2
            w = odd_w_in[o_idx]
            wp = jnp.concatenate([w, jnp.zeros((D_MODEL, ODD_COLS_PAD - w.shape[1]), F32)], axis=1).astype(BF16)
            lng = gmlp_ln_g[o_idx][None, :]
            lnb = gmlp_ln_b[o_idx][None, :]
            wmo = odd_w_out[o_idx].astype(BF16)
            ws = gmlp_w_s[o_idx]
            bs = gmlp_b_s[o_idx]
            tm = TOKEN_TILE
            kv_shape = (2, D_KV, HD)

            hp, u, v, q, ckv, skv, wkv, nsa_kv, gates = _odd_pre(yp, nrm, win_bf[i, 0], wout_bf[i, 0], wp, lng, lnb,
                                                                 tab_p)
            kcd = _cmp_means(ckv.reshape(bsz, seq, 2 * D_KV * HD))
            od = _nsa_prompt(q.reshape(bsz, seq, 1024), kcd, nsa_kv.reshape(bsz, seq, 1024),
                             gates.reshape(bsz, seq, 3 * D_HEADS))
            ws_p = jnp.tile(ws, (1, tm // CHUNK, tm // CHUNK))
            bs_p = jnp.tile(jnp.repeat(bs.T, C_GW, axis=1), (tm // CHUNK, 1))
            yp = _odd_post(u, v, od.reshape(tp, 512), ws_p, bs_p, wmo, hp, nrm, win_bf[i, 1], wout_bf[i, 1], CHUNK)
            win_rows = min(WINDOW, seq)
            outs["ckv_p"].append(ckv.reshape((bsz, seq) + kv_shape))
            outs["skv_p"].append(skv.reshape((bsz, seq) + kv_shape))
            outs["wkv_p"].append(wkv.reshape((bsz, seq) + kv_shape)[:, seq - win_rows:])

            hs, u, v, q, ckv, skv, wkv, _, gates = _odd_pre(ys, nrm, win_bf[i, 0], wout_bf[i, 0], wp, lng, lnb, tab_s)
            q3 = q.reshape(nb, nt, 1024)
            kv2 = 2 * D_KV * HD
            od_raw = _nsa_sample_attn(o_idx, page_table, _nsa_rows(q3[:, :, :512]), _nsa_rows(q3[:, :, 512:]),
                                      gates.reshape(nb, nt * D_HEADS, 3), ckv.reshape(nb, nt, kv2),
                                      skv.reshape(nb, nt, kv2), wkv.reshape(nb, nt, kv2), wstate, cc, cs)
            od4 = od_raw.reshape(nb, nt, D_HEADS, 2, HD)
            first = (jnp.arange(D_HEADS) < D_REP)[None, None, :, None]
            od = jnp.where(first, od4[:, :, :, 0], od4[:, :, :, 1]).reshape(ts, 512)
            ws_s = jnp.tile(ws[:, :nt, :nt], (1, tm // nt, tm // nt))
            bs_s = jnp.tile(jnp.repeat(bs[:, :nt].T, C_GW, axis=1), (tm // nt, 1))
            ys = _odd_post(u, v, od, ws_s, bs_s, wmo, hs, nrm, win_bf[i, 1], wout_bf[i, 1], nt)
            outs["ckv_s"].append(ckv.reshape((nb, nt) + kv_shape))
            outs["skv_s"].append(skv.reshape((nb, nt) + kv_shape))
            outs["wkv_s"].append(wkv.reshape((nb, nt) + kv_shape))
            outs["gv_s"].append(v.reshape(nb, nt, C_CH))

    st = lambda k: jnp.stack(outs[k], axis=0)
    return (yp.reshape(bsz, seq, D_MODEL), ys.reshape(nb, nt, D_MODEL), st("dkv_p"), st("dkv_s"), st("fkv_p"),
            st("fkv_s"), st("fl_p"), st("fl_s"), st("ckv_p"), st("ckv_s"), st("skv_p"), st("skv_s"), st("wkv_p"),
            st("wkv_s"), st("gv_s"))
```

```python
import functools
import math

import jax
import jax.numpy as jnp
from jax import lax
from jax.experimental import pallas as pl
from jax.experimental.pallas import tpu as pltpu

F32 = jnp.float32
BF16 = jnp.bfloat16

D_MODEL = 1024
DEPTH = 4
HD = 64
ROT_DIM = HD // 4
ROPE_THETA = 500000.0
A_HEADS = 4
B_HEADS = 8
C_CH = 512
C_GROUPS = 4
C_GW = C_CH // C_GROUPS
CHUNK = 128
D_HEADS = 8
D_KV = 2
D_REP = D_HEADS // D_KV
CMP_BLOCK = 32
SEL_BLOCK = 64
N_SEL = 8
WINDOW = 512
D_FF = 2816
EPS = 1e-6
NEG = -1e30
TINY = 1e-30
PAGE_SIZE = 128
PAST_LEN = 2048
N_PAGES = PAST_LEN // PAGE_SIZE
N_PAST_SEL = PAST_LEN // SEL_BLOCK
N_PAST_CMP = PAST_LEN // CMP_BLOCK
LOG2E = 1.4426950408889634
QSCALE = HD ** -0.5 * LOG2E
BIG = 1e30

LANES = 128
HALF = LANES // 2
VMEM_LIMIT = 56 * 2 ** 20
TOKEN_TILE = 256
FFN_CHUNK = D_FF // 2
EVEN_COLS_PAD = 3200
ODD_COLS_PAD = 2432
NT_DIMS = (((1,), (1,)), ((), ()))
HIGHEST = lax.Precision.HIGHEST


def _cparams(sem=None):
    return pltpu.CompilerParams(dimension_semantics=sem, vmem_limit_bytes=VMEM_LIMIT)


def _resident(shape):
    nd = len(shape)
    return pl.BlockSpec(shape, lambda *_: (0,) * nd, pipeline_mode=pl.Buffered(1))


def _rms(x, g):
    return x * lax.rsqrt(jnp.mean(x * x, axis=-1, keepdims=True) + EPS) * g


def _lane_tile(t, width):
    return jnp.concatenate([t] * (width // LANES), axis=1)


def _rope(x, c, s1, s2):
    w = x.shape[1]
    return (x * _lane_tile(c, w) + pltpu.roll(x, w - ROT_DIM // 2, 1) * _lane_tile(s1, w)
            + pltpu.roll(x, ROT_DIM // 2, 1) * _lane_tile(s2, w))


def _dup_halves(x):
    r = pltpu.roll(x, HALF, 1)
    lo = lax.broadcasted_iota(jnp.int32, x.shape, 1) < HALF
    return jnp.where(lo, x, r), jnp.where(lo, r, x)


def _ffn_half(x, g_pre, g_post, win_ref, wout_ref):
    xn = _rms(x, g_pre).astype(BF16)
    acc = None
    for c in range(D_FF // FFN_CHUNK):
        lo = c * FFN_CHUNK
        gate = jnp.dot(xn, win_ref[:, lo:lo + FFN_CHUNK], preferred_element_type=F32)
        up = jnp.dot(xn, win_ref[:, D_FF + lo:D_FF + lo + FFN_CHUNK], preferred_element_type=F32)
        act = (jax.nn.silu(gate) * up).astype(BF16)
        part = jnp.dot(act, wout_ref[lo:lo + FFN_CHUNK, :], preferred_element_type=F32)
        acc = part if acc is None else acc + part
    return x + 0.5 * _rms(acc, g_post)


def _cumsum_lanes(x):
    n = x.shape[1]
    lane = lax.broadcasted_iota(jnp.int32, x.shape, 1)
    s = 1
    while s < n:
        x = x + jnp.where(lane >= s, pltpu.roll(x, s, 1), 0.0)
        s *= 2
    return x


def _topk_rows(score, k):
    n = score.shape[0]
    row = lax.broadcasted_iota(jnp.int32, score.shape, 0).astype(F32)
    sel = jnp.zeros(score.shape, F32)
    for _ in range(k):
        m = jnp.max(score, axis=0, keepdims=True)
        first = jnp.min(jnp.where(score == m, row, float(n)), axis=0, keepdims=True)
        pick = row == first
        sel = jnp.where(pick, 1.0, sel)
        score = jnp.where(pick, -2.0, score)
    return sel


def _masked_softmax_rows(s, mask):
    s = jnp.where(mask, s, NEG)
    m = jnp.max(s, axis=-1, keepdims=True)
    p = jnp.where(mask, jnp.exp2(s - m), 0.0)
    return p / jnp.maximum(jnp.sum(p, axis=-1, keepdims=True), TINY)


def _even_pre_body(x_ref, nrm_ref, win_ref, wout_ref, wp_ref, bf_ref, c_ref, s1_ref, s2_ref,
                   h_ref, q_ref, dkv_ref, fkv_ref, k_ref, v_ref, logf_ref):
    h = _ffn_half(x_ref[...], nrm_ref[0:1, :], nrm_ref[1:2, :], win_ref, wout_ref)
    h_ref[...] = h
    hn = _rms(h, nrm_ref[2:3, :]).astype(BF16)
    z = jnp.dot(hn, wp_ref[...], preferred_element_type=F32)
    c, s1, s2 = c_ref[...], s1_ref[...], s2_ref[...]
    w = A_HEADS * 2 * HD
    aq = _rope(z[:, 0:w], c, s1, s2)
    bq = z[:, w:2 * w]
    ak = _rope(z[:, 2 * w:3 * w], c, s1, s2)
    av = z[:, 3 * w:4 * w]
    bk = z[:, 4 * w:5 * w]
    bv = z[:, 5 * w:6 * w]
    q_ref[:, 0:w] = (aq * QSCALE).astype(BF16)
    q_ref[:, w:2 * w] = (bq * QSCALE).astype(BF16)
    dkv_ref[:, 0:w] = ak
    dkv_ref[:, w:2 * w] = av
    fkv_ref[:, 0:w] = bk
    fkv_ref[:, w:2 * w] = bv
    k_ref[:, 0:w] = ak.astype(BF16)
    k_ref[:, w:2 * w] = bk.astype(BF16)
    v_ref[:, 0:w] = av.astype(BF16)
    v_ref[:, w:2 * w] = bv.astype(BF16)
    lf = jax.nn.log_sigmoid(z[:, 6 * w:6 * w + LANES] + bf_ref[...])
    logf_ref[...] = lf[:, 0:B_HEADS]


def _odd_pre_body(x_ref, nrm_ref, win_ref, wout_ref, wp_ref, lng_ref, lnb_ref, c_ref, s1_ref, s2_ref,
                  h_ref, u_ref, v_ref, q_ref, cmp_ref, sel_ref, win_kv_ref, nsa_ref, gate_ref):
    h = _ffn_half(x_ref[...], nrm_ref[0:1, :], nrm_ref[1:2, :], win_ref, wout_ref)
    h_ref[...] = h
    hn = _rms(h, nrm_ref[2:3, :]).astype(BF16)
    z = jnp.dot(hn, wp_ref[...], preferred_element_type=F32)
    c, s1, s2 = c_ref[...], s1_ref[...], s2_ref[...]
    u_ref[...] = jax.nn.gelu(z[:, 0:C_CH])
    gv = jax.nn.gelu(z[:, C_CH:2 * C_CH])
    for g in range(C_GROUPS):
        blk = gv[:, g * C_GW:(g + 1) * C_GW]
        mu = jnp.mean(blk, axis=-1, keepdims=True)
        var = jnp.mean(jnp.square(blk - mu), axis=-1, keepdims=True)
        v_ref[:, g * C_GW:(g + 1) * C_GW] = ((blk - mu) * lax.rsqrt(var + EPS) * lng_ref[:, g * C_GW:(g + 1) * C_GW]
                                             + lnb_ref[:, g * C_GW:(g + 1) * C_GW])
    o = 2 * C_CH
    wq = D_HEADS * HD
    q = z[:, o:o + wq]
    q_ref[:, 0:wq] = (q * QSCALE).astype(BF16)
    q_ref[:, wq:2 * wq] = (_rope(q, c, s1, s2) * QSCALE).astype(BF16)
    o += wq
    kvw = D_KV * HD
    ck, cv = z[:, o:o + kvw], z[:, o + kvw:o + 2 * kvw]
    sk, sv = _rope(z[:, o + 2 * kvw:o + 3 * kvw], c, s1, s2), z[:, o + 3 * kvw:o + 4 * kvw]
    wk, wv = _rope(z[:, o + 4 * kvw:o + 5 * kvw], c, s1, s2), z[:, o + 5 * kvw:o + 6 * kvw]
    cmp_ref[:, 0:kvw] = ck
    cmp_ref[:, kvw:2 * kvw] = cv
    sel_ref[:, 0:kvw] = sk
    sel_ref[:, kvw:2 * kvw] = sv
    win_kv_ref[:, 0:kvw] = wk
    win_kv_ref[:, kvw:2 * kvw] = wv
    for j, arr in enumerate((sk, sv, wk, wv)):
        d0, d1 = _dup_halves(arr)
        nsa_ref[:, (2 * j) * LANES:(2 * j + 1) * LANES] = d0.astype(BF16)
        nsa_ref[:, (2 * j + 1) * LANES:(2 * j + 2) * LANES] = d1.astype(BF16)
    o += 6 * kvw
    gate_ref[...] = jax.nn.sigmoid(z[:, o:o + LANES])[:, 0:3 * D_HEADS]


def _mixer_post(o, wmo_ref, h, nrm_ref, win_ref, wout_ref):
    m = jnp.dot(o.astype(BF16), wmo_ref[...], preferred_element_type=F32)
    h2 = h + _rms(m, nrm_ref[3:4, :])
    return _ffn_half(h2, nrm_ref[4:5, :], nrm_ref[5:6, :], win_ref, wout_ref)


def _even_post_body(oa_ref, ob_ref, sg_ref, wmo_ref, h_ref, nrm_ref, win_ref, wout_ref, y_ref, *, out_scale):
    sg = sg_ref[...]
    parts = []
    for hd in range(A_HEADS):
        blk = oa_ref[:, hd * 2 * HD:(hd + 1) * 2 * HD]
        parts.append(_rms(blk, sg) * out_scale)
    parts.append(ob_ref[...])
    o = jnp.concatenate(parts, axis=1)
    y_ref[...] = _mixer_post(o, wmo_ref, h_ref[...], nrm_ref, win_ref, wout_ref)


def _odd_post_body(u_ref, v_ref, od_ref, ws_ref, bs_ref, wmo_ref, h_ref, nrm_ref, win_ref, wout_ref, y_ref,
                   *, period):
    tm = u_ref.shape[0]
    r = lax.broadcasted_iota(jnp.int32, (tm, tm), 0)
    cidx = lax.broadcasted_iota(jnp.int32, (tm, tm), 1)
    keep = (r // period == cidx // period) & (cidx % period <= r % period)
    parts = []
    for g in range(C_GROUPS):
        wm = jnp.where(keep, ws_ref[g], 0.0).astype(BF16)
        vg = v_ref[:, g * C_GW:(g + 1) * C_GW].astype(BF16)
        mix = jnp.dot(wm, vg, preferred_element_type=F32) + bs_ref[:, g * C_GW:(g + 1) * C_GW]
        parts.append(u_ref[:, g * C_GW:(g + 1) * C_GW] * mix)
    parts.append(od_ref[...])
    o = jnp.concatenate(parts, axis=1)
    y_ref[...] = _mixer_post(o, wmo_ref, h_ref[...], nrm_ref, win_ref, wout_ref)


def _row_spec(tm, width):
    return pl.BlockSpec((tm, width), lambda i: (i, 0))


def _pair_slot(i, nblk):
    half = nblk // 2
    qi = i % nblk
    return (i // nblk) * nblk + jnp.where(qi < half, qi, half + nblk - 1 - qi)


def _slot_spec(tm, width, nblk):
    if nblk is None:
        return _row_spec(tm, width)
    return pl.BlockSpec((tm, width), lambda i: (_pair_slot(i, nblk), 0))


def _table_spec(tm, rows):
    nrep = rows // tm
    return pl.BlockSpec((tm, LANES), lambda i: (i % nrep, 0))


def _even_pre(x, nrm, win, wout, wp, bf_pad, tables, nblk=None):
    t = x.shape[0]
    tm = TOKEN_TILE
    tabs = [_table_spec(tm, tables[0].shape[0])] * 3
    outs = (jax.ShapeDtypeStruct((t, D_MODEL), F32), jax.ShapeDtypeStruct((t, 1024), BF16),
            jax.ShapeDtypeStruct((t, 1024), F32), jax.ShapeDtypeStruct((t, 1024), F32),
            jax.ShapeDtypeStruct((t, 1024), BF16), jax.ShapeDtypeStruct((t, 1024), BF16),
            jax.ShapeDtypeStruct((t, B_HEADS), F32))
    return pl.pallas_call(
        _even_pre_body, out_shape=outs, grid=(t // tm,),
        in_specs=[_row_spec(tm, D_MODEL), _resident(nrm.shape), _resident(win.shape), _resident(wout.shape),
                  _resident(wp.shape), _resident(bf_pad.shape)] + tabs,
        out_specs=(_row_spec(tm, D_MODEL), _slot_spec(tm, 1024, nblk), _row_spec(tm, 1024), _row_spec(tm, 1024),
                   _row_spec(tm, 1024), _row_spec(tm, 1024), _row_spec(tm, B_HEADS)),
        compiler_params=_cparams(("arbitrary",)), name="even_pre")(x, nrm, win, wout, wp, bf_pad, *tables)


def _odd_pre(x, nrm, win, wout, wp, lng, lnb, tables, nblk=None):
    t = x.shape[0]
    tm = TOKEN_TILE
    tabs = [_table_spec(tm, tables[0].shape[0])] * 3
    kv2 = 2 * D_KV * HD
    outs = (jax.ShapeDtypeStruct((t, D_MODEL), F32), jax.ShapeDtypeStruct((t, C_CH), F32),
            jax.ShapeDtypeStruct((t, C_CH), F32), jax.ShapeDtypeStruct((t, 1024), BF16),
            jax.ShapeDtypeStruct((t, kv2), F32), jax.ShapeDtypeStruct((t, kv2), F32),
            jax.ShapeDtypeStruct((t, kv2), F32), jax.ShapeDtypeStruct((t, 1024), BF16),
            jax.ShapeDtypeStruct((t, 3 * D_HEADS), F32))
    return pl.pallas_call(
        _odd_pre_body, out_shape=outs, grid=(t // tm,),
        in_specs=[_row_spec(tm, D_MODEL), _resident(nrm.shape), _resident(win.shape), _resident(wout.shape),
                  _resident(wp.shape), _resident(lng.shape), _resident(lnb.shape)] + tabs,
        out_specs=(_row_spec(tm, D_MODEL), _row_spec(tm, C_CH), _row_spec(tm, C_CH), _slot_spec(tm, 1024, nblk),
                   _row_spec(tm, kv2), _row_spec(tm, kv2), _row_spec(tm, kv2), _row_spec(tm, 1024),
                   _slot_spec(tm, 3 * D_HEADS, nblk)),
        compiler_params=_cparams(("arbitrary",)), name="odd_pre")(x, nrm, win, wout, wp, lng, lnb, *tables)


def _even_post(oa, ob, sg, wmo, h, nrm, win, wout, out_scale, nblk=None):
    t = h.shape[0]
    tm = TOKEN_TILE
    return pl.pallas_call(
        functools.partial(_even_post_body, out_scale=out_scale),
        out_shape=jax.ShapeDtypeStruct((t, D_MODEL), F32), grid=(t // tm,),
        in_specs=[_slot_spec(tm, 512, nblk), _slot_spec(tm, 512, nblk), _resident(sg.shape), _resident(wmo.shape),
                  _row_spec(tm, D_MODEL), _resident(nrm.shape), _resident(win.shape), _resident(wout.shape)],
        out_specs=_row_spec(tm, D_MODEL),
        compiler_params=_cparams(("arbitrary",)), name="even_post")(oa, ob, sg, wmo, h, nrm, win, wout)


def _odd_post(u, v, od, ws_tiled, bs_full, wmo, h, nrm, win, wout, period, nblk=None):
    t = h.shape[0]
    tm = TOKEN_TILE
    return pl.pallas_call(
        functools.partial(_odd_post_body, period=period),
        out_shape=jax.ShapeDtypeStruct((t, D_MODEL), F32), grid=(t // tm,),
        in_specs=[_row_spec(tm, C_CH), _row_spec(tm, C_CH), _slot_spec(tm, 512, nblk), _resident(ws_tiled.shape),
                  _resident(bs_full.shape), _resident(wmo.shape), _row_spec(tm, D_MODEL), _resident(nrm.shape),
                  _resident(win.shape), _resident(wout.shape)],
        out_specs=_row_spec(tm, D_MODEL),
        compiler_params=_cparams(("arbitrary",)), name="odd_post")(u, v, od, ws_tiled, bs_full, wmo, h, nrm, win,
                                                                   wout)


def _lambda_value(lam_ref, lam_init):
    lp = lam_ref[...]
    a = jnp.sum(lp[0:1, :] * lp[1:2, :], axis=-1, keepdims=True)
    b = jnp.sum(lp[2:3, :] * lp[3:4, :], axis=-1, keepdims=True)
    return jnp.exp(a) - jnp.exp(b) + lam_init


def _cumsum_body(x_ref, o_ref):
    o_ref[...] = _cumsum_lanes(x_ref[...]) * LOG2E


def _fold(x, op):
    out = x[:, 0:LANES]
    for t in range(1, x.shape[1] // LANES):
        out = op(out, x[:, t * LANES:(t + 1) * LANES])
    return out


def _paired_causal_softmax_pv(qpair, score_fn, v_fn, p, nblk, diag_mask):
    blocks = (p, nblk - 1 - p)
    steps = []
    for u in range(nblk - 1):
        is_a = u < p
        steps.append((is_a, jnp.where(is_a, u, u - p)))
    sd = [jnp.where(diag_mask, score_fn(qpair[i], blocks[i]), NEG) for i in range(2)]
    mrun = [_fold(sd[i], jnp.maximum) for i in range(2)]
    for is_a, kb in steps:
        f = _fold(score_fn(jnp.where(is_a, qpair[0], qpair[1]), kb), jnp.maximum)
        mrun[0] = jnp.where(is_a, jnp.maximum(mrun[0], f), mrun[0])
        mrun[1] = jnp.where(is_a, mrun[1], jnp.maximum(mrun[1], f))
    m = [jnp.max(mrun[i], axis=-1, keepdims=True) for i in range(2)]
    lrun, acc = [], []
    for i in range(2):
        pr = jnp.exp2(sd[i] - m[i])
        acc.append(jnp.dot(pr.astype(BF16), v_fn(blocks[i]), preferred_element_type=F32))
        lrun.append(_fold(pr, jnp.add))
    for is_a, kb in steps:
        s = score_fn(jnp.where(is_a, qpair[0], qpair[1]), kb)
        pr = jnp.exp2(s - jnp.where(is_a, m[0], m[1]))
        pv = jnp.dot(pr.astype(BF16), v_fn(kb), preferred_element_type=F32)
        f = _fold(pr, jnp.add)
        lrun[0] = jnp.where(is_a, lrun[0] + f, lrun[0])
        lrun[1] = jnp.where(is_a, lrun[1], lrun[1] + f)
        acc[0] = jnp.where(is_a, acc[0] + pv, acc[0])
        acc[1] = jnp.where(is_a, acc[1], acc[1] + pv)
    return [acc[i] / jnp.maximum(jnp.sum(lrun[i], axis=-1, keepdims=True), TINY) for i in range(2)]


def _cumsum_rows_of_lanes(x):
    return pl.pallas_call(_cumsum_body, out_shape=jax.ShapeDtypeStruct(x.shape, F32), name="fox_cumsum",
                          compiler_params=_cparams())(x)


def _pair_attn_body(*refs, fox, lam_init, nblk):
    q_ref, k_ref, v_ref, extra_ref, o_ref = refs
    tq = q_ref.shape[1]
    p = pl.program_id(2)
    lo = lax.broadcasted_iota(jnp.int32, (tq, LANES), 1) < HALF
    diag_mask = (lax.broadcasted_iota(jnp.int32, (tq, tq), 1) <= lax.broadcasted_iota(jnp.int32, (tq, tq), 0))

    def v_fn(kb):
        return v_ref[pl.ds(pl.multiple_of(kb * tq, tq), tq), :]

    outs = []
    for c in range(2):
        def score_fn(qh, kb, c=c):
            ks = pl.multiple_of(kb * tq, tq)
            s = lax.dot_general(qh, k_ref[pl.ds(ks, tq), :], NT_DIMS, preferred_element_type=F32)
            if fox:
                s = s - extra_ref[c:c + 1, pl.ds(ks, tq)]
            return s

        qpair = []
        for i in range(2):
            x = q_ref[i]
            zero = jnp.zeros_like(x)
            qpair.append(jnp.where(lo, x, zero) if c == 0 else jnp.where(lo, zero, x))
        outs.append(_paired_causal_softmax_pv(qpair, score_fn, v_fn, p, nblk, diag_mask))
    for i in range(2):
        if fox:
            o_ref[i] = jnp.where(lo, outs[0][i], outs[1][i])
        else:
            o_ref[i] = outs[0][i] - _lambda_value(extra_ref, lam_init) * outs[1][i]


def _pair_attn_prompt(q, k, v, extra, *, fox, lam_init=0.0):
    b, s, _ = k.shape
    tq = TOKEN_TILE
    nblk = s // tq
    off = 4 if fox else 0
    q4 = q.reshape(b, 2, s // 2, q.shape[-1])
    in_specs = [pl.BlockSpec((None, 2, tq, LANES), lambda bi, j, p: (bi, 0, p, j + off)),
                pl.BlockSpec((None, s, LANES), lambda bi, j, p: (bi, 0, j + off)),
                pl.BlockSpec((None, s, LANES), lambda bi, j, p: (bi, 0, j + off))]
    if fox:
        in_specs += [pl.BlockSpec((None, None, 2, s), lambda bi, j, p: (bi, j, 0, 0))]
    else:
        in_specs += [pl.BlockSpec(extra.shape, lambda bi, j, p: (0, 0))]
    out = pl.pallas_call(
        functools.partial(_pair_attn_body, fox=fox, lam_init=lam_init, nblk=nblk),
        out_shape=jax.ShapeDtypeStruct((b, 2, s // 2, 512), F32), grid=(b, 4, nblk // 2),
        in_specs=in_specs,
        out_specs=pl.BlockSpec((None, 2, tq, LANES), lambda bi, j, p: (bi, 0, p, j)),
        compiler_params=_cparams(("arbitrary", "arbitrary", "arbitrary")),
        name="fox_prompt" if fox else "diff_prompt")(q4, k, v, extra)
    return out.reshape(b * s, 512)


def _paged_attend(q_ref, pages, score_page, pv_page, new_ref, bias_past, bias_new, t_row):
    w = q_ref.shape[1]
    qb = q_ref[...]
    s = jnp.concatenate([score_page(qb, pg) for pg in pages], axis=1)
    if bias_past is not None:
        s = s + bias_past
    qf = qb.astype(F32)
    nt = new_ref.shape[0]
    s_new = []
    for j in range(nt):
        sj = jnp.sum(qf * new_ref[j:j + 1, 0:w], axis=-1, keepdims=True)
        if bias_new is not None:
            sj = sj + bias_new[j]
        s_new.append(jnp.where(t_row >= j, sj, NEG))
    m = jnp.max(s, axis=-1, keepdims=True)
    for sj in s_new:
        m = jnp.maximum(m, sj)
    p = jnp.exp2(s - m)
    p_new = [jnp.where(t_row >= j, jnp.exp2(s_new[j] - m), 0.0) for j in range(nt)]
    l = jnp.sum(p, axis=-1, keepdims=True)
    for pj in p_new:
        l = l + pj
    pb = p.astype(BF16)
    acc = None
    for i, pg in enumerate(pages):
        part = pv_page(pb[:, i * PAGE_SIZE:(i + 1) * PAGE_SIZE], pg)
        acc = part if acc is None else acc + part
    for j in range(nt):
        acc = acc + p_new[j] * new_ref[j:j + 1, w:2 * w]
    return acc / jnp.maximum(l, TINY)


def _diff_page_slab(pg, first):
    return jnp.concatenate([pg[pl.ds(first + hd, PAGE_SIZE, stride=2 * A_HEADS), :] for hd in range(A_HEADS)],
                           axis=1).astype(BF16)


def _diff_score_page(q, pg):
    return lax.dot_general(q, _diff_page_slab(pg, 0), NT_DIMS, preferred_element_type=F32)


def _diff_pv_page(p, pg):
    return jnp.dot(p, _diff_page_slab(pg, A_HEADS), preferred_element_type=F32)


def _fox_score_page(q, pg):
    return jnp.dot(q, pg[0:B_HEADS * HD, :].astype(BF16), preferred_element_type=F32)


def _fox_pv_page(p, pg):
    return lax.dot_general(p, pg[B_HEADS * HD:2 * B_HEADS * HD, :].astype(BF16), NT_DIMS,
                           preferred_element_type=F32)


def _even_sample_body(pt_ref, qa_ref, qb_ref, dnew_ref, fnew_ref, lnew_ref, lam_ref, *rest, lam_init):
    del pt_ref
    dpages = rest[0:N_PAGES]
    fpages = rest[N_PAGES:2 * N_PAGES]
    lpages = rest[2 * N_PAGES:3 * N_PAGES]
    o_ref = rest[3 * N_PAGES]
    nt = dnew_ref.shape[0]
    nrow = qa_ref.shape[0]
    w = qa_ref.shape[1]
    rid = lax.broadcasted_iota(jnp.int32, (nrow, 1), 0)
    t_row = rid // 8
    h_row = rid % 8
    lane = lax.broadcasted_iota(jnp.int32, (nrow, w), 1)

    res = _paged_attend(qa_ref, dpages, _diff_score_page, _diff_pv_page, dnew_ref, None, None, t_row)
    lam = _lambda_value(lam_ref, lam_init)
    coef = jnp.where(h_row % 2 == 0, 1.0, -lam)
    wgt = jnp.where(lane // (2 * HD) == h_row // 2, coef, 0.0)
    o_ref[:, 0:w] = jnp.sum((res * wgt).reshape(nt, 8, w), axis=1)

    lt = jnp.concatenate([pg[...] for pg in lpages], axis=1)
    cum = _cumsum_lanes(lt)
    tot = jnp.concatenate([jnp.sum(lt, axis=1, keepdims=True)] * nt, axis=0)
    lnew = lnew_ref[...]
    tix = lax.broadcasted_iota(jnp.int32, lnew.shape, 0)
    cnew = jnp.zeros_like(lnew)
    for j in range(nt):
        cnew = cnew + jnp.where(tix >= j, lnew[j:j + 1, :], 0.0)
    hsel = lax.broadcasted_iota(jnp.int32, (nrow, B_HEADS), 1) == h_row
    cnew_rows = jnp.broadcast_to(cnew[:, None, :], (nt, 8, B_HEADS)).reshape(nrow, B_HEADS)
    fq_new = jnp.sum(jnp.where(hsel, cnew_rows, 0.0), axis=-1, keepdims=True)
    bias_past = ((tot + fq_new) - jnp.concatenate([cum] * nt, axis=0)) * LOG2E
    bias_new = [(fq_new - jnp.sum(jnp.where(hsel, cnew[j:j + 1, :], 0.0), axis=-1, keepdims=True)) * LOG2E
                for j in range(nt)]
    res = _paged_attend(qb_ref, fpages, _fox_score_page, _fox_pv_page, fnew_ref, bias_past, bias_new, t_row)
    wgt = jnp.where(lane // HD == h_row, 1.0, 0.0)
    o_ref[:, w:2 * w] = jnp.sum((res * wgt).reshape(nt, 8, w), axis=1)


def _even_sample_attn(e, page_table, qbd_a, qbd_b, dkv_new, fkv_new, lnew, lam_p, cache_d, cache_f, cache_l,
                      lam_init):
    nb, nt = dkv_new.shape[0], dkv_new.shape[1]

    def per_b(shape):
        nd = len(shape)
        return pl.BlockSpec((None,) + tuple(shape[1:]), lambda b, pt: (b,) + (0,) * (nd - 1))

    def page(rows, p):
        return pl.BlockSpec((None, None, rows, PAGE_SIZE), lambda b, pt: (e, pt[b, p], 0, 0))

    in_specs = [per_b(qbd_a.shape), per_b(qbd_b.shape), per_b(dkv_new.shape), per_b(fkv_new.shape),
                per_b(lnew.shape), pl.BlockSpec(lam_p.shape, lambda b, pt: (0, 0))]
    in_specs += [page(cache_d.shape[2], p) for p in range(N_PAGES)]
    in_specs += [page(cache_f.shape[2], p) for p in range(N_PAGES)]
    in_specs += [page(B_HEADS, p) for p in range(N_PAGES)]
    gs = pltpu.PrefetchScalarGridSpec(
        num_scalar_prefetch=1, grid=(nb,), in_specs=in_specs,
        out_specs=pl.BlockSpec((None, nt, 1024), lambda b, pt: (b, 0, 0)))
    return pl.pallas_call(
        functools.partial(_even_sample_body, lam_init=lam_init),
        out_shape=jax.ShapeDtypeStruct((nb, nt, 1024), F32), grid_spec=gs,
        compiler_params=_cparams(("arbitrary",)), name="even_sample")(
            page_table, qbd_a, qbd_b, dkv_new, fkv_new, lnew, lam_p,
            *([cache_d] * N_PAGES), *([cache_f] * N_PAGES), *([cache_l] * N_PAGES))


def _cmp_means_body(x_ref, o_ref):
    x = x_ref[...]
    n = x.shape[0] // CMP_BLOCK
    m = jnp.sum(x.reshape(n, CMP_BLOCK, x.shape[1]), axis=1) * (1.0 / CMP_BLOCK)
    for j in range(2):
        d0, d1 = _dup_halves(m[:, j * LANES:(j + 1) * LANES])
        o_ref[:, (2 * j) * LANES:(2 * j + 1) * LANES] = d0.astype(BF16)
        o_ref[:, (2 * j + 1) * LANES:(2 * j + 2) * LANES] = d1.astype(BF16)


def _cmp_means(cmp_kv):
    b, s, w = cmp_kv.shape
    return pl.pallas_call(
        _cmp_means_body, out_shape=jax.ShapeDtypeStruct((b, s // CMP_BLOCK, 2 * w), BF16), grid=(b,),
        in_specs=[pl.BlockSpec((None, s, w), lambda i: (i, 0, 0))],
        out_specs=pl.BlockSpec((None, s // CMP_BLOCK, 2 * w), lambda i: (i, 0, 0)),
        compiler_params=_cparams(("arbitrary",)), name="nsa_cmp_means")(cmp_kv)


def _nsa_prompt_body(q_ref, gate_ref, kc_ref, kv_ref, oh_ref, o_ref, *, nblk):
    nq = q_ref.shape[1]
    s_len = kv_ref.shape[0]
    n_cmp = kc_ref.shape[0]
    n_sel = s_len // SEL_BLOCK
    p = pl.program_id(1)
    blocks = (p, nblk - 1 - p)
    nrow = D_REP * nq
    wq = D_HEADS * HD
    lo_q = lax.broadcasted_iota(jnp.int32, (nq, LANES), 1) < HALF
    row_q = lax.broadcasted_iota(jnp.int32, (nrow, 1), 0) % nq
    diag_mask = lax.broadcasted_iota(jnp.int32, (nrow, nq), 1) <= row_q
    pr_i = lax.broadcasted_iota(jnp.int32, (n_sel, n_cmp), 0)
    pc_i = lax.broadcasted_iota(jnp.int32, (n_sel, n_cmp), 1)
    pair_t = jnp.where(pc_i // (SEL_BLOCK // CMP_BLOCK) == pr_i, 1.0, 0.0)
    blk = lax.broadcasted_iota(jnp.int32, (n_sel, nq), 0)
    qlane = lax.broadcasted_iota(jnp.int32, (n_sel, nq), 1)
    wlen = WINDOW + nq

    def stack_q(i, base, g):
        parts = []
        for rep in range(D_REP):
            hd = g * D_REP + rep
            slab = q_ref[i, :, base + (hd // 2) * LANES:base + (hd // 2 + 1) * LANES]
            keep = lo_q if hd % 2 == 0 else jnp.logical_not(lo_q)
            parts.append(jnp.where(keep, slab, jnp.zeros_like(slab)))
        return jnp.concatenate(parts, axis=0)

    for g in range(D_KV):
        kc = kc_ref[:, g * LANES:(g + 1) * LANES]
        vc = kc_ref[:, (2 + g) * LANES:(3 + g) * LANES]
        q_aug, o_c, o_w = [], [], []
        for i in range(2):
            q0 = blocks[i] * nq
            qpos_col = q0 + row_q
            qc = stack_q(i, 0, g)
            qr = stack_q(i, wq, g)
            sc = lax.dot_general(qc, kc, NT_DIMS, preferred_element_type=F32)
            c_end = lax.broadcasted_iota(jnp.int32, (nrow, n_cmp), 1) * CMP_BLOCK + (CMP_BLOCK - 1)
            pc = _masked_softmax_rows(sc, c_end <= qpos_col)
            o_c.append(jnp.dot(pc.astype(BF16), vc, preferred_element_type=F32))
            imp_c = pc[0:nq] + pc[nq:2 * nq] + pc[2 * nq:3 * nq] + pc[3 * nq:4 * nq]
            imp_t = lax.dot_general(pair_t, imp_c, NT_DIMS, precision=HIGHEST, preferred_element_type=F32)
            qp = q0 + qlane
            score = jnp.where(qp // SEL_BLOCK == blk, D_REP + 1.0, jnp.where(blk * SEL_BLOCK <= qp, imp_t, -1.0))
            sel_t = _topk_rows(score, min(N_SEL, n_sel))
            pen_t = jnp.concatenate([(sel_t - 1.0) * BIG, jnp.zeros((LANES - n_sel, nq), F32)], axis=0)
            pen = pen_t.T.astype(BF16)
            q_aug.append(jnp.concatenate([qr, jnp.concatenate([pen] * D_REP, axis=0)], axis=1))
            ws = pl.multiple_of(jnp.maximum(q0 - WINDOW, 0), nq)
            wk = kv_ref[pl.ds(ws, wlen), (4 + g) * LANES:(5 + g) * LANES]
            wv = kv_ref[pl.ds(ws, wlen), (6 + g) * LANES:(7 + g) * LANES]
            wpos = ws + lax.broadcasted_iota(jnp.int32, (nrow, wlen), 1)
            wmask = (wpos <= qpos_col) & (wpos > qpos_col - WINDOW)
            pw = _masked_softmax_rows(lax.dot_general(qr, wk, NT_DIMS, preferred_element_type=F32), wmask)
            o_w.append(jnp.dot(pw.astype(BF16), wv, preferred_element_type=F32))

        def score_fn(qh, kb, g=g):
            ks = pl.multiple_of(kb * nq, nq)
            k_aug = jnp.concatenate([kv_ref[pl.ds(ks, nq), g * LANES:(g + 1) * LANES], oh_ref[pl.ds(ks, nq), :]],
                                    axis=1)
            return lax.dot_general(qh, k_aug, NT_DIMS, preferred_element_type=F32)

        def v_fn(kb, g=g):
            return kv_ref[pl.ds(pl.multiple_of(kb * nq, nq), nq), (2 + g) * LANES:(3 + g) * LANES]

        o_s = _paired_causal_softmax_pv(q_aug, score_fn, v_fn, p, nblk, diag_mask)

        for i in range(2):
            gates = gate_ref[i]
            heads = []
            for rep in range(D_REP):
                hd = g * D_REP + rep
                rows = slice(rep * nq, (rep + 1) * nq)
                heads.append(gates[:, 3 * hd:3 * hd + 1] * o_c[i][rows] + gates[:, 3 * hd + 1:3 * hd + 2] * o_s[i][rows]
                             + gates[:, 3 * hd + 2:3 * hd + 3] * o_w[i][rows])
            for j in range(D_REP // 2):
                slab = g * (D_REP // 2) + j
                o_ref[i, :, slab * LANES:(slab + 1) * LANES] = jnp.where(lo_q, heads[2 * j], heads[2 * j + 1])


def _nsa_prompt(q, gates, kcd, nsa_kv, onehot):
    b, s, _ = nsa_kv.shape
    nq = TOKEN_TILE
    nblk = s // nq
    q4 = q.reshape(b, 2, s // 2, q.shape[-1])
    g4 = gates.reshape(b, 2, s // 2, gates.shape[-1])
    out = pl.pallas_call(
        functools.partial(_nsa_prompt_body, nblk=nblk),
        out_shape=jax.ShapeDtypeStruct((b, 2, s // 2, D_HEADS * HD), F32), grid=(b, nblk // 2),
        in_specs=[pl.BlockSpec((None, 2, nq, q.shape[-1]), lambda bi, p: (bi, 0, p, 0)),
                  pl.BlockSpec((None, 2, nq, gates.shape[-1]), lambda bi, p: (bi, 0, p, 0)),
                  pl.BlockSpec((None,) + kcd.shape[1:], lambda bi, p: (bi, 0, 0)),
                  pl.BlockSpec((None, s, nsa_kv.shape[-1]), lambda bi, p: (bi, 0, 0)),
                  pl.BlockSpec(onehot.shape, lambda bi, p: (0, 0))],
        out_specs=pl.BlockSpec((None, 2, nq, D_HEADS * HD), lambda bi, p: (bi, 0, p, 0)),
        compiler_params=_cparams(("arbitrary", "arbitrary")), name="nsa_prompt")(q4, g4, kcd, nsa_kv, onehot)
    return out.reshape(b * s, D_HEADS * HD)


def _nsa_sample_body(pt_ref, qc_ref, qr_ref, gate_ref, cnew_ref, snew_ref, wnew_ref, wst_ref, avg_ref, *rest):
    del pt_ref, cnew_ref
    cpages = rest[0:N_PAGES]
    spages = rest[N_PAGES:2 * N_PAGES]
    o_ref = rest[2 * N_PAGES]
    nrow = qc_ref.shape[0]
    nt = snew_ref.shape[0]
    rid = lax.broadcasted_iota(jnp.int32, (nrow, 1), 0)
    t_row = rid // D_HEADS
    qc = qc_ref[...]
    qr = qr_ref[...]
    qrf = qr.astype(F32)

    cmp_t = jnp.concatenate([pg[...].astype(BF16) for pg in cpages], axis=1)
    means_t = jnp.dot(cmp_t, avg_ref[...], preferred_element_type=F32)
    kc_t = means_t[0:LANES].astype(BF16)
    vc_t = means_t[LANES:2 * LANES].astype(BF16)
    sc = jnp.dot(qc, kc_t, preferred_element_type=F32)
    mc = jnp.max(sc, axis=-1, keepdims=True)
    pc = jnp.exp2(sc - mc)
    pc = pc / jnp.maximum(jnp.sum(pc, axis=-1, keepdims=True), TINY)
    o_c = lax.dot_general(pc.astype(BF16), vc_t, NT_DIMS, preferred_element_type=F32)

    ncol = nt * D_KV
    gr = lax.broadcasted_iota(jnp.int32, (ncol, nrow), 0)
    gc = lax.broadcasted_iota(jnp.int32, (ncol, nrow), 1)
    gsum = jnp.where(gc // D_REP == gr, 1.0, 0.0)
    imp_c = jnp.dot(gsum, pc, precision=HIGHEST, preferred_element_type=F32)
    pr = lax.broadcasted_iota(jnp.int32, (N_PAST_SEL, N_PAST_CMP), 0)
    pcx = lax.broadcasted_iota(jnp.int32, (N_PAST_SEL, N_PAST_CMP), 1)
    pair_t = jnp.where(pcx // (SEL_BLOCK // CMP_BLOCK) == pr, 1.0, 0.0)
    imp_t = lax.dot_general(pair_t, imp_c, NT_DIMS, precision=HIGHEST, preferred_element_type=F32)
    sel_t = _topk_rows(imp_t, N_SEL - 1).astype(BF16)
    er = lax.broadcasted_iota(jnp.int32, (nrow, ncol), 0)
    ec = lax.broadcasted_iota(jnp.int32, (nrow, ncol), 1)
    g_t = jnp.where(er // D_REP == ec, 1.0, 0.0).astype(BF16)
    sel_r = lax.dot_general(g_t, sel_t, NT_DIMS, preferred_element_type=F32)
    eb_r = lax.broadcasted_iota(jnp.int32, (N_PAST_SEL, PAST_LEN), 0)
    eb_c = lax.broadcasted_iota(jnp.int32, (N_PAST_SEL, PAST_LEN), 1)
    eb = jnp.where(eb_c // SEL_BLOCK == eb_r, 1.0, 0.0).astype(BF16)
    smask = jnp.dot(sel_r.astype(BF16), eb, preferred_element_type=F32) > 0.5

    def new_rows(s_past, mask_past, new_ref):
        s_past = jnp.where(mask_past, s_past, NEG)
        s_new = [jnp.where(t_row >= j, jnp.sum(qrf * new_ref[j:j + 1, 0:LANES], axis=-1, keepdims=True), NEG)
                 for j in range(nt)]
        m = jnp.max(s_past, axis=-1, keepdims=True)
        for sj in s_new:
            m = jnp.maximum(m, sj)
        p = jnp.where(mask_past, jnp.exp2(s_past - m), 0.0)
        p_new = [jnp.where(t_row >= j, jnp.exp2(s_new[j] - m), 0.0) for j in range(nt)]
        l = jnp.sum(p, axis=-1, keepdims=True)
        for pj in p_new:
            l = l + pj
        inv = 1.0 / jnp.maximum(l, TINY)
        return p * inv, [pj * inv for pj in p_new]

    s_parts = [jnp.dot(qr, pg[0:LANES, :].astype(BF16), preferred_element_type=F32) for pg in spages]
    ps, ps_new = new_rows(jnp.concatenate(s_parts, axis=1), smask, snew_ref)
    psb = ps.astype(BF16)
    o_s = None
    for i, pg in enumerate(spages):
        part = lax.dot_general(psb[:, i * PAGE_SIZE:(i + 1) * PAGE_SIZE], pg[LANES:2 * LANES, :].astype(BF16),
                               NT_DIMS, preferred_element_type=F32)
        o_s = part if o_s is None else o_s + part
    for j in range(nt):
        o_s = o_s + ps_new[j] * snew_ref[j:j + 1, LANES:2 * LANES]

    wb = wst_ref.shape[1]
    wk_t = wst_ref[0:LANES, :].astype(BF16)
    wv_t = wst_ref[LANES:2 * LANES, :].astype(BF16)
    jpos = lax.broadcasted_iota(jnp.int32, (nrow, wb), 1)
    wmask = jpos > t_row - (WINDOW - wb)
    pw, pw_new = new_rows(jnp.dot(qr, wk_t, preferred_element_type=F32), wmask, wnew_ref)
    o_w = lax.dot_general(pw.astype(BF16), wv_t, NT_DIMS, preferred_element_type=F32)
    for j in range(nt):
        o_w = o_w + pw_new[j] * wnew_ref[j:j + 1, LANES:2 * LANES]

    g = gate_ref[...]
    o_ref[...] = g[:, 0:1] * o_c + g[:, 1:2] * o_s + g[:, 2:3] * o_w


def _nsa_sample_attn(o, page_table, qc_bd, qr_bd, gates, cnew, snew, wnew, win_state, cache_c, cache_s):
    nb = qc_bd.shape[0]
    kv2 = 2 * D_KV * HD
    key = jnp.arange(PAST_LEN)[:, None] // CMP_BLOCK
    avg = jnp.where(key == jnp.arange(N_PAST_CMP)[None, :], 1.0 / CMP_BLOCK, 0.0).astype(BF16)

    def per_b(shape):
        nd = len(shape)
        return pl.BlockSpec((None,) + tuple(shape[1:]), lambda b, pt: (b,) + (0,) * (nd - 1))

    def page(p):
        return pl.BlockSpec((None, None, kv2, PAGE_SIZE), lambda b, pt: (o, pt[b, p], 0, 0))

    in_specs = [per_b(qc_bd.shape), per_b(qr_bd.shape), per_b(gates.shape), per_b(cnew.shape), per_b(snew.shape),
                per_b(wnew.shape),
                pl.BlockSpec((None, None) + win_state.shape[2:], lambda b, pt: (o, b, 0, 0)),
                pl.BlockSpec(avg.shape, lambda b, pt: (0, 0))]
    in_specs += [page(p) for p in range(N_PAGES)]
    in_specs += [page(p) for p in range(N_PAGES)]
    gs = pltpu.PrefetchScalarGridSpec(
        num_scalar_prefetch=1, grid=(nb,), in_specs=in_specs,
        out_specs=pl.BlockSpec((None,) + qc_bd.shape[1:], lambda b, pt: (b, 0, 0)))
    return pl.pallas_call(
        _nsa_sample_body, out_shape=jax.ShapeDtypeStruct(qc_bd.shape, F32), grid_spec=gs,
        compiler_params=_cparams(("arbitrary",)), name="nsa_sample")(
            page_table, qc_bd, qr_bd, gates, cnew, snew, wnew, win_state, avg,
            *([cache_c] * N_PAGES), *([cache_s] * N_PAGES))


def _rope_tables(pos):
    half = ROT_DIM // 2
    inv = 1.0 / (ROPE_THETA ** (jnp.arange(half, dtype=F32) * 2.0 / ROT_DIM))
    ang = pos.astype(F32)[:, None] * inv[None, :]
    cos, sin = jnp.cos(ang), jnp.sin(ang)
    n = pos.shape[0]
    zeros8 = jnp.zeros((n, half), F32)
    rest0 = jnp.zeros((n, HD - ROT_DIM), F32)
    c = jnp.concatenate([cos, cos, jnp.ones((n, HD - ROT_DIM), F32)], axis=1)
    s1 = jnp.concatenate([-sin, zeros8, rest0], axis=1)
    s2 = jnp.concatenate([zeros8, sin, rest0], axis=1)
    return tuple(jnp.concatenate([t, t], axis=1) for t in (c, s1, s2))


def _block_diag_rows(q, group):
    nb, nt, w = q.shape
    keep = (jnp.arange(w)[None, :] // HD) == jnp.arange(8)[:, None]
    return jnp.where(keep[None, None], q[:, :, None, :], jnp.zeros((), q.dtype)).reshape(nb, nt * 8, w)


def _nsa_rows(q):
    nb, nt, _ = q.shape
    qh = q.reshape(nb, nt, D_HEADS, HD)
    z = jnp.zeros_like(qh)
    first = (jnp.arange(D_HEADS) < D_REP)[None, None, :, None]
    out = jnp.concatenate([jnp.where(first, qh, z), jnp.where(first, z, qh)], axis=-1)
    return out.reshape(nb, nt * D_HEADS, 2 * HD)


def kernel(x_prompt, x_sample, cache_diff_kv, cache_fox_kv, cache_fox_logf, cache_nsa_cmp_kv, cache_nsa_sel_kv,
           state_nsa_win_kv, page_table, norm_g, ffn_w_in, ffn_w_out, even_w_in, even_w_out, fox_b_f, diff_lambda,
           diff_subln_g, odd_w_in, odd_w_out, gmlp_ln_g, gmlp_ln_b, gmlp_w_s, gmlp_b_s):
    bsz, seq, _ = x_prompt.shape
    nb, nt, _ = x_sample.shape
    tp = bsz * seq
    ts = nb * nt
    n_pool = cache_diff_kv.shape[1]

    yp = x_prompt.reshape(tp, D_MODEL)
    ys = x_sample.reshape(ts, D_MODEL)
    tab_p = _rope_tables(jnp.arange(seq))
    tab_s = _rope_tables(PAST_LEN + (jnp.arange(ts) % nt))
    win_bf = ffn_w_in.astype(BF16)
    wout_bf = ffn_w_out.astype(BF16)
    nblk_p = seq // TOKEN_TILE
    sel_onehot = (jnp.arange(seq)[:, None] // SEL_BLOCK == jnp.arange(LANES)[None, :]).astype(BF16)

    kv2 = 2 * D_KV * HD
    keys_last = (0, 1, 3, 4, 5, 2)
    cd = cache_diff_kv.reshape(cache_diff_kv.shape[0], n_pool, PAGE_SIZE * 2 * A_HEADS, 2 * HD)
    cf = cache_fox_kv.transpose(keys_last).reshape(cache_fox_kv.shape[0], n_pool, 2 * B_HEADS * HD, PAGE_SIZE)
    cl = cache_fox_logf.transpose(0, 1, 3, 2)
    cc = cache_nsa_cmp_kv.transpose(keys_last).reshape(cache_nsa_cmp_kv.shape[0], n_pool, kv2, PAGE_SIZE)
    cs = cache_nsa_sel_kv.transpose(keys_last).reshape(cache_nsa_sel_kv.shape[0], n_pool, kv2, PAGE_SIZE)
    wstate = state_nsa_win_kv.transpose(keys_last).reshape(state_nsa_win_kv.shape[0], nb, kv2,
                                                           state_nsa_win_kv.shape[2])

    w512 = 512
    outs = {k: [] for k in ("dkv_p", "dkv_s", "fkv_p", "fkv_s", "fl_p", "fl_s", "ckv_p", "ckv_s", "skv_p", "skv_s",
                            "wkv_p", "wkv_s", "gv_s")}
    for i in range(DEPTH):
        nrm = norm_g[i]
        if i % 2 == 0:
            e = i // 2
            lam_init = 0.8 - 0.6 * math.exp(-0.3 * i)
            w = even_w_in[e]
            aq, ak, av, bq, bk, bv, bfc = jnp.split(w, [512, 1024, 1536, 2048, 2560, 3072], axis=1)
            wp = jnp.concatenate([aq, bq, ak, av, bk, bv, bfc,
                                  jnp.zeros((D_MODEL, EVEN_COLS_PAD - w.shape[1]), F32)], axis=1).astype(BF16)
            bf_pad = jnp.concatenate([fox_b_f[e], jnp.zeros((LANES - B_HEADS,), F32)])[None, :]
            lam_p = diff_lambda[e]
            sg = diff_subln_g[e][None, :]
            wmo = even_w_out[e].astype(BF16)

            hp, q, dkv, fkv, kb, vb, logf = _even_pre(yp, nrm, win_bf[i, 0], wout_bf[i, 0], wp, bf_pad, tab_p,
                                                      nblk=nblk_p)
            k3, v3 = (a.reshape(bsz, seq, 1024) for a in (kb, vb))
            oa = _pair_attn_prompt(q, k3, v3, lam_p, fox=False, lam_init=lam_init)
            logf_t = logf.reshape(bsz, seq, B_HEADS).transpose(0, 2, 1).reshape(bsz * B_HEADS, seq)
            frow = _cumsum_rows_of_lanes(logf_t).reshape(bsz, B_HEADS // 2, 2, seq)
            ob = _pair_attn_prompt(q, k3, v3, frow, fox=True)
            yp = _even_post(oa, ob, sg, wmo, hp, nrm, win_bf[i, 1], wout_bf[i, 1], 1.0 - lam_init, nblk=nblk_p)
            outs["dkv_p"].append(dkv.reshape(bsz, seq, 2, A_HEADS, 2 * HD))
            outs["fkv_p"].append(fkv.reshape(bsz, seq, 2, B_HEADS, HD))
            outs["fl_p"].append(logf.reshape(bsz, seq, B_HEADS))

            hs, q, dkv, fkv, kb, vb, logf = _even_pre(ys, nrm, win_bf[i, 0], wout_bf[i, 0], wp, bf_pad, tab_s)
            qs3 = q.reshape(nb, nt, 1024)
            o = _even_sample_attn(e, page_table, _block_diag_rows(qs3[:, :, :w512], 8),
                                  _block_diag_rows(qs3[:, :, w512:], 8), dkv.reshape(nb, nt, 1024),
                                  fkv.reshape(nb, nt, 1024), logf.reshape(nb, nt, B_HEADS), lam_p, cd, cf,
                                  cl, lam_init).reshape(ts, 1024)
            ys = _even_post(o[:, :w512], o[:, w512:], sg, wmo, hs, nrm, win_bf[i, 1], wout_bf[i, 1], 1.0 - lam_init)
            outs["dkv_s"].append(dkv.reshape(nb, nt, 2, A_HEADS, 2 * HD))
            outs["fkv_s"].append(fkv.reshape(nb, nt, 2, B_HEADS, HD))
            outs["fl_s"].append(logf.reshape(nb, nt, B_HEADS))
        else:
            o_idx = i // 2
            w = odd_w_in[o_idx]
            wp = jnp.concatenate([w, jnp.zeros((D_MODEL, ODD_COLS_PAD - w.shape[1]), F32)], axis=1).astype(BF16)
            lng = gmlp_ln_g[o_idx][None, :]
            lnb = gmlp_ln_b[o_idx][None, :]
            wmo = odd_w_out[o_idx].astype(BF16)
            ws = gmlp_w_s[o_idx]
            bs = gmlp_b_s[o_idx]
            tm = TOKEN_TILE
            kv_shape = (2, D_KV, HD)

            hp, u, v, q, ckv, skv, wkv, nsa_kv, gates = _odd_pre(yp, nrm, win_bf[i, 0], wout_bf[i, 0], wp, lng, lnb,
                                                                 tab_p, nblk=nblk_p)
            kcd = _cmp_means(ckv.reshape(bsz, seq, 2 * D_KV * HD))
            od = _nsa_prompt(q, gates, kcd, nsa_kv.reshape(bsz, seq, 1024), sel_onehot)
            ws_p = jnp.tile(ws, (1, tm // CHUNK, tm // CHUNK))
            bs_p = jnp.tile(jnp.repeat(bs.T, C_GW, axis=1), (tm // CHUNK, 1))
            yp = _odd_post(u, v, od, ws_p, bs_p, wmo, hp, nrm, win_bf[i, 1], wout_bf[i, 1], CHUNK, nblk=nblk_p)
            win_rows = min(WINDOW, seq)
            outs["ckv_p"].append(ckv.reshape((bsz, seq) + kv_shape))
            outs["skv_p"].append(skv.reshape((bsz, seq) + kv_shape))
            outs["wkv_p"].append(wkv.reshape((bsz, seq) + kv_shape)[:, seq - win_rows:])

            hs, u, v, q, ckv, skv, wkv, _, gates = _odd_pre(ys, nrm, win_bf[i, 0], wout_bf[i, 0], wp, lng, lnb, tab_s)
            q3 = q.reshape(nb, nt, 1024)
            kv2 = 2 * D_KV * HD
            od_raw = _nsa_sample_attn(o_idx, page_table, _nsa_rows(q3[:, :, :512]), _nsa_rows(q3[:, :, 512:]),
                                      gates.reshape(nb, nt * D_HEADS, 3), ckv.reshape(nb, nt, kv2),
                                      skv.reshape(nb, nt, kv2), wkv.reshape(nb, nt, kv2), wstate, cc, cs)
            od4 = od_raw.reshape(nb, nt, D_HEADS, 2, HD)
            first = (jnp.arange(D_HEADS) < D_REP)[None, None, :, None]
            od = jnp.where(first, od4[:, :, :, 0], od4[:, :, :, 1]).reshape(ts, 512)
            ws_s = jnp.tile(ws[:, :nt, :nt], (1, tm // nt, tm // nt))
            bs_s = jnp.tile(jnp.repeat(bs[:, :nt].T, C_GW, axis=1), (tm // nt, 1))
            ys = _odd_post(u, v, od, ws_s, bs_s, wmo, hs, nrm, win_bf[i, 1], wout_bf[i, 1], nt)
            outs["ckv_s"].append(ckv.reshape((nb, nt) + kv_shape))
            outs["skv_s"].append(skv.reshape((nb, nt) + kv_shape))
            outs["wkv_s"].append(wkv.reshape((nb, nt) + kv_shape))
            outs["gv_s"].append(v.reshape(nb, nt, C_CH))

    st = lambda k: jnp.stack(outs[k], axis=0)
    return (yp.reshape(bsz, seq, D_MODEL), ys.reshape(nb, nt, D_MODEL), st("dkv_p"), st("dkv_s"), st("fkv_p"),
            st("fkv_s"), st("fl_p"), st("fl_s"), st("ckv_p"), st("ckv_s"), st("skv_p"), st("skv_s"), st("wkv_p"),
            st("wkv_s"), st("gv_s"))
```

```python
import functools
import math

import jax
import jax.numpy as jnp
from jax import lax
from jax.experimental import pallas as pl
from jax.experimental.pallas import tpu as pltpu

F32 = jnp.float32
BF16 = jnp.bfloat16

D_MODEL = 1024
DEPTH = 4
HD = 64
ROT_DIM = HD // 4
ROPE_THETA = 500000.0
A_HEADS = 4
B_HEADS = 8
C_CH = 512
C_GROUPS = 4
C_GW = C_CH // C_GROUPS
CHUNK = 128
D_HEADS = 8
D_KV = 2
D_REP = D_HEADS // D_KV
CMP_BLOCK = 32
SEL_BLOCK = 64
N_SEL = 8
WINDOW = 512
D_FF = 2816
EPS = 1e-6
NEG = -1e30
TINY = 1e-30
PAGE_SIZE = 128
PAST_LEN = 2048
N_PAGES = PAST_LEN // PAGE_SIZE
N_PAST_SEL = PAST_LEN // SEL_BLOCK
N_PAST_CMP = PAST_LEN // CMP_BLOCK
LOG2E = 1.4426950408889634
QSCALE = HD ** -0.5 * LOG2E
BIG = 1e30

LANES = 128
HALF = LANES // 2
VMEM_LIMIT = 56 * 2 ** 20
TOKEN_TILE = 256
FFN_CHUNK = D_FF // 2
EVEN_COLS_PAD = 3200
ODD_COLS_PAD = 2432
NSA_DECODE_ROWS = 2
NT_DIMS = (((1,), (1,)), ((), ()))
HIGHEST = lax.Precision.HIGHEST


def _cparams(sem=None):
    return pltpu.CompilerParams(dimension_semantics=sem, vmem_limit_bytes=VMEM_LIMIT)


def _resident(shape):
    nd = len(shape)
    return pl.BlockSpec(shape, lambda *_: (0,) * nd, pipeline_mode=pl.Buffered(1))


def _rms(x, g):
    return x * lax.rsqrt(jnp.mean(x * x, axis=-1, keepdims=True) + EPS) * g


def _lane_tile(t, width):
    return jnp.concatenate([t] * (width // LANES), axis=1)


def _rope(x, c, s1, s2):
    w = x.shape[1]
    return (x * _lane_tile(c, w) + pltpu.roll(x, w - ROT_DIM // 2, 1) * _lane_tile(s1, w)
            + pltpu.roll(x, ROT_DIM // 2, 1) * _lane_tile(s2, w))


def _dup_halves(x):
    r = pltpu.roll(x, HALF, 1)
    lo = lax.broadcasted_iota(jnp.int32, x.shape, 1) < HALF
    return jnp.where(lo, x, r), jnp.where(lo, r, x)


def _ffn_half(x, g_pre, g_post, win_ref, wout_ref):
    xn = _rms(x, g_pre).astype(BF16)
    acc = None
    for c in range(D_FF // FFN_CHUNK):
        lo = c * FFN_CHUNK
        gate = jnp.dot(xn, win_ref[:, lo:lo + FFN_CHUNK], preferred_element_type=F32)
        up = jnp.dot(xn, win_ref[:, D_FF + lo:D_FF + lo + FFN_CHUNK], preferred_element_type=F32)
        act = (jax.nn.silu(gate) * up).astype(BF16)
        part = jnp.dot(act, wout_ref[lo:lo + FFN_CHUNK, :], preferred_element_type=F32)
        acc = part if acc is None else acc + part
    return x + 0.5 * _rms(acc, g_post)


def _cumsum_lanes(x):
    n = x.shape[1]
    lane = lax.broadcasted_iota(jnp.int32, x.shape, 1)
    s = 1
    while s < n:
        x = x + jnp.where(lane >= s, pltpu.roll(x, s, 1), 0.0)
        s *= 2
    return x


def _topk_rows(score, k):
    n = score.shape[0]
    row = lax.broadcasted_iota(jnp.int32, score.shape, 0).astype(F32)
    sel = jnp.zeros(score.shape, F32)
    for _ in range(k):
        m = jnp.max(score, axis=0, keepdims=True)
        first = jnp.min(jnp.where(score == m, row, float(n)), axis=0, keepdims=True)
        pick = row == first
        sel = jnp.where(pick, 1.0, sel)
        score = jnp.where(pick, -2.0, score)
    return sel


def _masked_softmax_rows(s, mask):
    s = jnp.where(mask, s, NEG)
    m = jnp.max(s, axis=-1, keepdims=True)
    p = jnp.where(mask, jnp.exp2(s - m), 0.0)
    return p / jnp.maximum(jnp.sum(p, axis=-1, keepdims=True), TINY)


def _even_pre_body(*refs, final_layout):
    x_ref, nrm_ref, win_ref, wout_ref, wp_ref, bf_ref, c_ref, s1_ref, s2_ref = refs[:9]
    h_ref, q_ref, dkv_ref, fkv_ref, k_ref, v_ref, logf_ref = refs[-7:]
    tm = x_ref.shape[0]
    h = _ffn_half(x_ref[...], nrm_ref[0:1, :], nrm_ref[1:2, :], win_ref, wout_ref)
    h_ref[...] = h
    hn = _rms(h, nrm_ref[2:3, :]).astype(BF16)
    z = jnp.dot(hn, wp_ref[...], preferred_element_type=F32)
    c, s1, s2 = c_ref[...], s1_ref[...], s2_ref[...]
    w = A_HEADS * 2 * HD
    aq = _rope(z[:, 0:w], c, s1, s2)
    bq = z[:, w:2 * w]
    ak = _rope(z[:, 2 * w:3 * w], c, s1, s2)
    av = z[:, 3 * w:4 * w]
    bk = z[:, 4 * w:5 * w]
    bv = z[:, 5 * w:6 * w]
    q_ref[:, 0:w] = (aq * QSCALE).astype(BF16)
    q_ref[:, w:2 * w] = (bq * QSCALE).astype(BF16)
    if final_layout:
        for hd in range(A_HEADS):
            dkv_ref[pl.ds(hd, tm, stride=2 * A_HEADS), :] = ak[:, hd * 2 * HD:(hd + 1) * 2 * HD]
            dkv_ref[pl.ds(A_HEADS + hd, tm, stride=2 * A_HEADS), :] = av[:, hd * 2 * HD:(hd + 1) * 2 * HD]
        fkv_ref[0:w, :] = bk.T
        fkv_ref[w:2 * w, :] = bv.T
    else:
        dkv_ref[:, 0:w] = ak
        dkv_ref[:, w:2 * w] = av
        fkv_ref[:, 0:w] = bk
        fkv_ref[:, w:2 * w] = bv
    k_ref[:, 0:w] = ak.astype(BF16)
    k_ref[:, w:2 * w] = bk.astype(BF16)
    v_ref[:, 0:w] = av.astype(BF16)
    v_ref[:, w:2 * w] = bv.astype(BF16)
    lf = jax.nn.log_sigmoid(z[:, 6 * w:6 * w + LANES] + bf_ref[...])
    logf_ref[...] = lf[:, 0:B_HEADS]


def _odd_pre_body(*refs, final_layout):
    x_ref, nrm_ref, win_ref, wout_ref, wp_ref, lng_ref, lnb_ref, c_ref, s1_ref, s2_ref = refs[:10]
    if final_layout:
        h_ref, u_ref, v_ref, q_ref, kcd_ref, cmp_ref, sel_ref, win_kv_ref, nsa_ref, gate_ref = refs[-10:]
    else:
        h_ref, u_ref, v_ref, q_ref, cmp_ref, sel_ref, win_kv_ref, gate_ref = refs[-8:]
    h = _ffn_half(x_ref[...], nrm_ref[0:1, :], nrm_ref[1:2, :], win_ref, wout_ref)
    h_ref[...] = h
    hn = _rms(h, nrm_ref[2:3, :]).astype(BF16)
    z = jnp.dot(hn, wp_ref[...], preferred_element_type=F32)
    c, s1, s2 = c_ref[...], s1_ref[...], s2_ref[...]
    u_ref[...] = jax.nn.gelu(z[:, 0:C_CH])
    gv = jax.nn.gelu(z[:, C_CH:2 * C_CH])
    for g in range(C_GROUPS):
        blk = gv[:, g * C_GW:(g + 1) * C_GW]
        mu = jnp.mean(blk, axis=-1, keepdims=True)
        var = jnp.mean(jnp.square(blk - mu), axis=-1, keepdims=True)
        v_ref[:, g * C_GW:(g + 1) * C_GW] = ((blk - mu) * lax.rsqrt(var + EPS) * lng_ref[:, g * C_GW:(g + 1) * C_GW]
                                             + lnb_ref[:, g * C_GW:(g + 1) * C_GW])
    o = 2 * C_CH
    wq = D_HEADS * HD
    q = z[:, o:o + wq]
    q_ref[:, 0:wq] = (q * QSCALE).astype(BF16)
    q_ref[:, wq:2 * wq] = (_rope(q, c, s1, s2) * QSCALE).astype(BF16)
    o += wq
    kvw = D_KV * HD
    ck, cv = z[:, o:o + kvw], z[:, o + kvw:o + 2 * kvw]
    sk, sv = _rope(z[:, o + 2 * kvw:o + 3 * kvw], c, s1, s2), z[:, o + 3 * kvw:o + 4 * kvw]
    wk, wv = _rope(z[:, o + 4 * kvw:o + 5 * kvw], c, s1, s2), z[:, o + 5 * kvw:o + 6 * kvw]
    if final_layout:
        tm = x_ref.shape[0]
        ckv = jnp.concatenate([ck, cv], axis=1)
        cmp_ref[...] = ckv.T
        sel_ref[...] = jnp.concatenate([sk, sv], axis=1).T
        win_kv_ref[...] = jnp.concatenate([wk, wv], axis=1).T
        means = jnp.sum(ckv.reshape(tm // CMP_BLOCK, CMP_BLOCK, 2 * kvw), axis=1) * (1.0 / CMP_BLOCK)
        for j in range(2):
            d0, d1 = _dup_halves(means[:, j * LANES:(j + 1) * LANES])
            kcd_ref[:, (2 * j) * LANES:(2 * j + 1) * LANES] = d0
            kcd_ref[:, (2 * j + 1) * LANES:(2 * j + 2) * LANES] = d1
        for j, arr in enumerate((sk, sv, wk, wv)):
            d0, d1 = _dup_halves(arr)
            nsa_ref[:, (2 * j) * LANES:(2 * j + 1) * LANES] = d0.astype(BF16)
            nsa_ref[:, (2 * j + 1) * LANES:(2 * j + 2) * LANES] = d1.astype(BF16)
    else:
        cmp_ref[:, 0:kvw] = ck
        cmp_ref[:, kvw:2 * kvw] = cv
        sel_ref[:, 0:kvw] = sk
        sel_ref[:, kvw:2 * kvw] = sv
        win_kv_ref[:, 0:kvw] = wk
        win_kv_ref[:, kvw:2 * kvw] = wv
    o += 6 * kvw
    gate_ref[...] = jax.nn.sigmoid(z[:, o:o + LANES])[:, 0:3 * D_HEADS]


def _mixer_post(o, wmo_ref, h, nrm_ref, win_ref, wout_ref):
    m = jnp.dot(o.astype(BF16), wmo_ref[...], preferred_element_type=F32)
    h2 = h + _rms(m, nrm_ref[3:4, :])
    return _ffn_half(h2, nrm_ref[4:5, :], nrm_ref[5:6, :], win_ref, wout_ref)


def _even_post_body(oa_ref, ob_ref, sg_ref, wmo_ref, h_ref, nrm_ref, win_ref, wout_ref, y_ref, *, out_scale):
    sg = sg_ref[...]
    parts = []
    for hd in range(A_HEADS):
        blk = oa_ref[:, hd * 2 * HD:(hd + 1) * 2 * HD]
        parts.append(_rms(blk, sg) * out_scale)
    parts.append(ob_ref[...])
    o = jnp.concatenate(parts, axis=1)
    y_ref[...] = _mixer_post(o, wmo_ref, h_ref[...], nrm_ref, win_ref, wout_ref)


def _odd_post_body(u_ref, v_ref, od_ref, ws_ref, bs_ref, wmo_ref, h_ref, nrm_ref, win_ref, wout_ref, y_ref,
                   *, chunked):
    tm = u_ref.shape[0]
    parts = []
    if chunked:
        keep = (lax.broadcasted_iota(jnp.int32, (CHUNK, CHUNK), 1) <= lax.broadcasted_iota(jnp.int32, (CHUNK, CHUNK), 0))
        for g in range(C_GROUPS):
            wm = jnp.where(keep, ws_ref[g], 0.0).astype(BF16)
            vg = v_ref[:, g * C_GW:(g + 1) * C_GW].astype(BF16)
            mix = jnp.concatenate([jnp.dot(wm, vg[ch * CHUNK:(ch + 1) * CHUNK], preferred_element_type=F32)
                                   for ch in range(tm // CHUNK)], axis=0) + bs_ref[:, g * C_GW:(g + 1) * C_GW]
            parts.append(u_ref[:, g * C_GW:(g + 1) * C_GW] * mix)
    else:
        v = v_ref[...]
        mix = ws_ref[0] * v + bs_ref[...]
        for k in range(1, ws_ref.shape[0]):
            mix = mix + ws_ref[k] * pltpu.roll(v, k, 0)
        parts.append(u_ref[...] * mix)
    parts.append(od_ref[...])
    o = jnp.concatenate(parts, axis=1)
    y_ref[...] = _mixer_post(o, wmo_ref, h_ref[...], nrm_ref, win_ref, wout_ref)


def _row_spec(tm, width):
    return pl.BlockSpec((tm, width), lambda i: (i, 0))


def _pair_slot(i, nblk):
    half = nblk // 2
    qi = i % nblk
    return (i // nblk) * nblk + jnp.where(qi < half, qi, half + nblk - 1 - qi)


def _slot_spec(tm, width, nblk):
    if nblk is None:
        return _row_spec(tm, width)
    return pl.BlockSpec((tm, width), lambda i: (_pair_slot(i, nblk), 0))


def _table_spec(tm, rows):
    nrep = rows // tm
    return pl.BlockSpec((tm, LANES), lambda i: (i % nrep, 0))


def _ffn_specs(win_all, wout_all, layer, half):
    def pick(shape):
        return pl.BlockSpec((None, None) + tuple(shape[2:]), lambda *_: (layer, half, 0, 0),
                            pipeline_mode=pl.Buffered(1))
    return [pick(win_all.shape), pick(wout_all.shape)]


def _stacked(layer, nlayers, prev, shape, block, index):
    spec = pl.BlockSpec((None,) + block, lambda i: (layer,) + index(i))
    return jax.ShapeDtypeStruct((nlayers,) + shape, F32), spec, prev


def _even_pre(x, nrm, win_all, wout_all, layer, wp, bf_pad, tables, nblk=None, prev=None):
    t = x.shape[0]
    tm = TOKEN_TILE
    tabs = [_table_spec(tm, tables[0].shape[0])] * 3
    final = nblk is not None
    sds = jax.ShapeDtypeStruct
    aliases, extra_in, extra_specs = {}, [], []
    if final:
        n_even, e, bsz = (DEPTH + 1) // 2, layer // 2, t // (nblk * tm)
        dkv = _stacked(e, n_even, None if prev is None else prev[0], (t * 2 * A_HEADS, 2 * HD),
                       (tm * 2 * A_HEADS, 2 * HD), lambda i: (i, 0))
        fkv = _stacked(e, n_even, None if prev is None else prev[1], (bsz, 2 * B_HEADS * HD, nblk * tm),
                       (None, 2 * B_HEADS * HD, tm), lambda i: (i // nblk, 0, i % nblk))
        for out_idx, st in ((2, dkv), (3, fkv)):
            if st[2] is not None:
                aliases[9 + len(extra_in)] = out_idx
                extra_in.append(st[2])
                extra_specs.append(pl.BlockSpec(memory_space=pl.ANY))
        kv_shapes, kv_specs = (dkv[0], fkv[0]), (dkv[1], fkv[1])
    else:
        kv_shapes = (sds((t, 1024), F32), sds((t, 1024), F32))
        kv_specs = (_row_spec(tm, 1024), _row_spec(tm, 1024))
    outs = (sds((t, D_MODEL), F32), sds((t, 1024), BF16)) + kv_shapes + (sds((t, 1024), BF16), sds((t, 1024), BF16),
                                                                        sds((t, B_HEADS), F32))
    return pl.pallas_call(
        functools.partial(_even_pre_body, final_layout=final), out_shape=outs, grid=(t // tm,),
        in_specs=[_row_spec(tm, D_MODEL), _resident(nrm.shape)] + _ffn_specs(win_all, wout_all, layer, 0)
        + [_resident(wp.shape), _resident(bf_pad.shape)] + tabs + extra_specs,
        out_specs=(_row_spec(tm, D_MODEL), _slot_spec(tm, 1024, nblk)) + kv_specs
        + (_row_spec(tm, 1024), _row_spec(tm, 1024), _row_spec(tm, B_HEADS)),
        input_output_aliases=aliases,
        compiler_params=_cparams(("arbitrary",)), name="even_pre")(x, nrm, win_all, wout_all, wp, bf_pad, *tables,
                                                                   *extra_in)


def _odd_pre(x, nrm, win_all, wout_all, layer, wp, lng, lnb, tables, nblk=None, prev=None):
    t = x.shape[0]
    tm = TOKEN_TILE
    tabs = [_table_spec(tm, tables[0].shape[0])] * 3
    kv2 = 2 * D_KV * HD
    final = nblk is not None
    sds = jax.ShapeDtypeStruct
    aliases, extra_in, extra_specs = {}, [], []
    head = (sds((t, D_MODEL), F32), sds((t, C_CH), F32), sds((t, C_CH), F32), sds((t, 1024), BF16))
    head_specs = (_row_spec(tm, D_MODEL), _row_spec(tm, C_CH), _row_spec(tm, C_CH), _slot_spec(tm, 1024, nblk))
    gate = (sds((t, 3 * D_HEADS), F32),)
    gate_spec = (_slot_spec(tm, 3 * D_HEADS, nblk),)
    if final:
        n_odd, o, bsz = DEPTH // 2, layer // 2, t // (nblk * tm)
        stacks = [_stacked(o, n_odd, None if prev is None else prev[j], (bsz, kv2, nblk * tm), (None, kv2, tm),
                           lambda i: (i // nblk, 0, i % nblk)) for j in range(3)]
        for j, st in enumerate(stacks):
            if st[2] is not None:
                aliases[10 + len(extra_in)] = 5 + j
                extra_in.append(st[2])
                extra_specs.append(pl.BlockSpec(memory_space=pl.ANY))
        outs = (head + (sds((t // CMP_BLOCK, 4 * LANES), F32),) + tuple(st[0] for st in stacks)
                + (sds((t, 1024), BF16),) + gate)
        out_specs = (head_specs + (_row_spec(tm // CMP_BLOCK, 4 * LANES),) + tuple(st[1] for st in stacks)
                     + (_row_spec(tm, 1024),) + gate_spec)
    else:
        outs = head + (sds((t, kv2), F32),) * 3 + gate
        out_specs = head_specs + (_row_spec(tm, kv2),) * 3 + gate_spec
    return pl.pallas_call(
        functools.partial(_odd_pre_body, final_layout=final), out_shape=outs, grid=(t // tm,),
        in_specs=[_row_spec(tm, D_MODEL), _resident(nrm.shape)] + _ffn_specs(win_all, wout_all, layer, 0)
        + [_resident(wp.shape), _resident(lng.shape), _resident(lnb.shape)] + tabs + extra_specs,
        out_specs=out_specs, input_output_aliases=aliases,
        compiler_params=_cparams(("arbitrary",)), name="odd_pre")(x, nrm, win_all, wout_all, wp, lng, lnb, *tables,
                                                                  *extra_in)


def _even_post(oa, ob, sg, wmo, h, nrm, win_all, wout_all, layer, out_scale, nblk=None):
    t = h.shape[0]
    tm = TOKEN_TILE
    return pl.pallas_call(
        functools.partial(_even_post_body, out_scale=out_scale),
        out_shape=jax.ShapeDtypeStruct((t, D_MODEL), F32), grid=(t // tm,),
        in_specs=[_slot_spec(tm, 512, nblk), _slot_spec(tm, 512, nblk), _resident(sg.shape), _resident(wmo.shape),
                  _row_spec(tm, D_MODEL), _resident(nrm.shape)] + _ffn_specs(win_all, wout_all, layer, 1),
        out_specs=_row_spec(tm, D_MODEL),
        compiler_params=_cparams(("arbitrary",)), name="even_post")(oa, ob, sg, wmo, h, nrm, win_all, wout_all)


def _odd_post(u, v, od, ws, bs_full, wmo, h, nrm, win_all, wout_all, layer, chunked, nblk=None):
    t = h.shape[0]
    tm = TOKEN_TILE
    return pl.pallas_call(
        functools.partial(_odd_post_body, chunked=chunked),
        out_shape=jax.ShapeDtypeStruct((t, D_MODEL), F32), grid=(t // tm,),
        in_specs=[_row_spec(tm, C_CH), _row_spec(tm, C_CH), _slot_spec(tm, 512, nblk), _resident(ws.shape),
                  _resident(bs_full.shape), _resident(wmo.shape), _row_spec(tm, D_MODEL), _resident(nrm.shape)]
        + _ffn_specs(win_all, wout_all, layer, 1),
        out_specs=_row_spec(tm, D_MODEL),
        compiler_params=_cparams(("arbitrary",)), name="odd_post")(u, v, od, ws, bs_full, wmo, h, nrm, win_all,
                                                                   wout_all)


def _lambda_value(lam_ref, lam_init):
    lp = lam_ref[...]
    a = jnp.sum(lp[0:1, :] * lp[1:2, :], axis=-1, keepdims=True)
    b = jnp.sum(lp[2:3, :] * lp[3:4, :], axis=-1, keepdims=True)
    return jnp.exp(a) - jnp.exp(b) + lam_init


def _cumsum_body(x_ref, o_ref):
    o_ref[...] = _cumsum_lanes(x_ref[...]) * LOG2E


def _fold(x, op):
    out = x[:, 0:LANES]
    for t in range(1, x.shape[1] // LANES):
        out = op(out, x[:, t * LANES:(t + 1) * LANES])
    return out


def _paired_causal_softmax_pv(qpair, score_fn, v_fn, p, nblk, diag_mask):
    blocks = (p, nblk - 1 - p)
    steps = []
    for u in range(nblk - 1):
        is_a = u < p
        steps.append((is_a, jnp.where(is_a, u, u - p)))
    sd = [jnp.where(diag_mask, score_fn(qpair[i], blocks[i]), NEG) for i in range(2)]
    mrun = [_fold(sd[i], jnp.maximum) for i in range(2)]
    for is_a, kb in steps:
        f = _fold(score_fn(jnp.where(is_a, qpair[0], qpair[1]), kb), jnp.maximum)
        mrun[0] = jnp.where(is_a, jnp.maximum(mrun[0], f), mrun[0])
        mrun[1] = jnp.where(is_a, mrun[1], jnp.maximum(mrun[1], f))
    m = [jnp.max(mrun[i], axis=-1, keepdims=True) for i in range(2)]
    lrun, acc = [], []
    for i in range(2):
        pr = jnp.exp2(sd[i] - m[i])
        acc.append(jnp.dot(pr.astype(BF16), v_fn(blocks[i]), preferred_element_type=F32))
        lrun.append(_fold(pr, jnp.add))
    for is_a, kb in steps:
        s = score_fn(jnp.where(is_a, qpair[0], qpair[1]), kb)
        pr = jnp.exp2(s - jnp.where(is_a, m[0], m[1]))
        pv = jnp.dot(pr.astype(BF16), v_fn(kb), preferred_element_type=F32)
        f = _fold(pr, jnp.add)
        lrun[0] = jnp.where(is_a, lrun[0] + f, lrun[0])
        lrun[1] = jnp.where(is_a, lrun[1], lrun[1] + f)
        acc[0] = jnp.where(is_a, acc[0] + pv, acc[0])
        acc[1] = jnp.where(is_a, acc[1], acc[1] + pv)
    return [acc[i] / jnp.maximum(jnp.sum(lrun[i], axis=-1, keepdims=True), TINY) for i in range(2)]


def _cumsum_rows_of_lanes(x):
    return pl.pallas_call(_cumsum_body, out_shape=jax.ShapeDtypeStruct(x.shape, F32), name="fox_cumsum",
                          compiler_params=_cparams())(x)


def _pair_attn_body(*refs, fox, lam_init, nblk):
    q_ref, k_ref, v_ref, extra_ref, o_ref = refs
    tq = q_ref.shape[1]
    p = pl.program_id(2)
    lo = lax.broadcasted_iota(jnp.int32, (tq, LANES), 1) < HALF
    diag_mask = (lax.broadcasted_iota(jnp.int32, (tq, tq), 1) <= lax.broadcasted_iota(jnp.int32, (tq, tq), 0))

    def v_fn(kb):
        return v_ref[pl.ds(pl.multiple_of(kb * tq, tq), tq), :]

    outs = []
    for c in range(2):
        def score_fn(qh, kb, c=c):
            ks = pl.multiple_of(kb * tq, tq)
            s = lax.dot_general(qh, k_ref[pl.ds(ks, tq), :], NT_DIMS, preferred_element_type=F32)
            if fox:
                s = s - extra_ref[c:c + 1, pl.ds(ks, tq)]
            return s

        qpair = []
        for i in range(2):
            x = q_ref[i]
            zero = jnp.zeros_like(x)
            qpair.append(jnp.where(lo, x, zero) if c == 0 else jnp.where(lo, zero, x))
        outs.append(_paired_causal_softmax_pv(qpair, score_fn, v_fn, p, nblk, diag_mask))
    for i in range(2):
        if fox:
            o_ref[i] = jnp.where(lo, outs[0][i], outs[1][i])
        else:
            o_ref[i] = outs[0][i] - _lambda_value(extra_ref, lam_init) * outs[1][i]


def _pair_attn_prompt(q, k, v, extra, *, fox, lam_init=0.0):
    b, s, _ = k.shape
    tq = TOKEN_TILE
    nblk = s // tq
    off = 4 if fox else 0
    q4 = q.reshape(b, 2, s // 2, q.shape[-1])
    in_specs = [pl.BlockSpec((None, 2, tq, LANES), lambda bi, j, p: (bi, 0, p, j + off)),
                pl.BlockSpec((None, s, LANES), lambda bi, j, p: (bi, 0, j + off)),
                pl.BlockSpec((None, s, LANES), lambda bi, j, p: (bi, 0, j + off))]
    if fox:
        in_specs += [pl.BlockSpec((None, None, 2, s), lambda bi, j, p: (bi, j, 0, 0))]
    else:
        in_specs += [pl.BlockSpec(extra.shape, lambda bi, j, p: (0, 0))]
    out = pl.pallas_call(
        functools.partial(_pair_attn_body, fox=fox, lam_init=lam_init, nblk=nblk),
        out_shape=jax.ShapeDtypeStruct((b, 2, s // 2, 512), F32), grid=(b, 4, nblk // 2),
        in_specs=in_specs,
        out_specs=pl.BlockSpec((None, 2, tq, LANES), lambda bi, j, p: (bi, 0, p, j)),
        compiler_params=_cparams(("arbitrary", "arbitrary", "arbitrary")),
        name="fox_prompt" if fox else "diff_prompt")(q4, k, v, extra)
    return out.reshape(b * s, 512)


def _paged_attend(q_ref, pages, score_page, pv_page, new_ref, bias_past, bias_new, t_row):
    w = q_ref.shape[1]
    qb = q_ref[...]
    s = jnp.concatenate([score_page(qb, pg) for pg in pages], axis=1)
    if bias_past is not None:
        s = s + bias_past
    qf = qb.astype(F32)
    nt = new_ref.shape[0]
    s_new = []
    for j in range(nt):
        sj = jnp.sum(qf * new_ref[j:j + 1, 0:w], axis=-1, keepdims=True)
        if bias_new is not None:
            sj = sj + bias_new[j]
        s_new.append(jnp.where(t_row >= j, sj, NEG))
    m = jnp.max(s, axis=-1, keepdims=True)
    for sj in s_new:
        m = jnp.maximum(m, sj)
    p = jnp.exp2(s - m)
    p_new = [jnp.where(t_row >= j, jnp.exp2(s_new[j] - m), 0.0) for j in range(nt)]
    l = jnp.sum(p, axis=-1, keepdims=True)
    for pj in p_new:
        l = l + pj
    pb = p.astype(BF16)
    acc = None
    for i, pg in enumerate(pages):
        part = pv_page(pb[:, i * PAGE_SIZE:(i + 1) * PAGE_SIZE], pg)
        acc = part if acc is None else acc + part
    for j in range(nt):
        acc = acc + p_new[j] * new_ref[j:j + 1, w:2 * w]
    return acc / jnp.maximum(l, TINY)


def _diff_page_slab(pg, first):
    return jnp.concatenate([pg[pl.ds(first + hd, PAGE_SIZE, stride=2 * A_HEADS), :] for hd in range(A_HEADS)],
                           axis=1).astype(BF16)


def _diff_score_page(q, pg):
    return lax.dot_general(q, _diff_page_slab(pg, 0), NT_DIMS, preferred_element_type=F32)


def _diff_pv_page(p, pg):
    return jnp.dot(p, _diff_page_slab(pg, A_HEADS), preferred_element_type=F32)


def _fox_score_page(q, pg):
    return jnp.dot(q, pg[0:B_HEADS * HD, :].astype(BF16), preferred_element_type=F32)


def _fox_pv_page(p, pg):
    return lax.dot_general(p, pg[B_HEADS * HD:2 * B_HEADS * HD, :].astype(BF16), NT_DIMS,
                           preferred_element_type=F32)


def _even_sample_body(pt_ref, qa_ref, qb_ref, dnew_ref, fnew_ref, lnew_ref, lam_ref, *rest, lam_init):
    del pt_ref
    dpages = rest[0:N_PAGES]
    fpages = rest[N_PAGES:2 * N_PAGES]
    lpages = rest[2 * N_PAGES:3 * N_PAGES]
    o_ref = rest[3 * N_PAGES]
    nt = dnew_ref.shape[0]
    nrow = qa_ref.shape[0]
    w = qa_ref.shape[1]
    rid = lax.broadcasted_iota(jnp.int32, (nrow, 1), 0)
    t_row = rid // 8
    h_row = rid % 8
    lane = lax.broadcasted_iota(jnp.int32, (nrow, w), 1)

    res = _paged_attend(qa_ref, dpages, _diff_score_page, _diff_pv_page, dnew_ref, None, None, t_row)
    lam = _lambda_value(lam_ref, lam_init)
    coef = jnp.where(h_row % 2 == 0, 1.0, -lam)
    wgt = jnp.where(lane // (2 * HD) == h_row // 2, coef, 0.0)
    o_ref[:, 0:w] = jnp.sum((res * wgt).reshape(nt, 8, w), axis=1)

    lt = jnp.concatenate([pg[...] for pg in lpages], axis=1)
    cum = _cumsum_lanes(lt)
    tot = jnp.concatenate([jnp.sum(lt, axis=1, keepdims=True)] * nt, axis=0)
    lnew = lnew_ref[...]
    tix = lax.broadcasted_iota(jnp.int32, lnew.shape, 0)
    cnew = jnp.zeros_like(lnew)
    for j in range(nt):
        cnew = cnew + jnp.where(tix >= j, lnew[j:j + 1, :], 0.0)
    hsel = lax.broadcasted_iota(jnp.int32, (nrow, B_HEADS), 1) == h_row
    cnew_rows = jnp.broadcast_to(cnew[:, None, :], (nt, 8, B_HEADS)).reshape(nrow, B_HEADS)
    fq_new = jnp.sum(jnp.where(hsel, cnew_rows, 0.0), axis=-1, keepdims=True)
    bias_past = ((tot + fq_new) - jnp.concatenate([cum] * nt, axis=0)) * LOG2E
    bias_new = [(fq_new - jnp.sum(jnp.where(hsel, cnew[j:j + 1, :], 0.0), axis=-1, keepdims=True)) * LOG2E
                for j in range(nt)]
    res = _paged_attend(qb_ref, fpages, _fox_score_page, _fox_pv_page, fnew_ref, bias_past, bias_new, t_row)
    wgt = jnp.where(lane // HD == h_row, 1.0, 0.0)
    o_ref[:, w:2 * w] = jnp.sum((res * wgt).reshape(nt, 8, w), axis=1)


def _even_sample_attn(e, page_table, qbd_a, qbd_b, dkv_new, fkv_new, lnew, lam_p, cache_d, cache_f, cache_l,
                      lam_init):
    nb, nt = dkv_new.shape[0], dkv_new.shape[1]

    def per_b(shape):
        nd = len(shape)
        return pl.BlockSpec((None,) + tuple(shape[1:]), lambda b, pt: (b,) + (0,) * (nd - 1))

    def page(rows, p):
        return pl.BlockSpec((None, None, rows, PAGE_SIZE), lambda b, pt: (e, pt[b, p], 0, 0))

    in_specs = [per_b(qbd_a.shape), per_b(qbd_b.shape), per_b(dkv_new.shape), per_b(fkv_new.shape),
                per_b(lnew.shape), pl.BlockSpec(lam_p.shape, lambda b, pt: (0, 0))]
    in_specs += [page(cache_d.shape[2], p) for p in range(N_PAGES)]
    in_specs += [page(cache_f.shape[2], p) for p in range(N_PAGES)]
    in_specs += [page(B_HEADS, p) for p in range(N_PAGES)]
    gs = pltpu.PrefetchScalarGridSpec(
        num_scalar_prefetch=1, grid=(nb,), in_specs=in_specs,
        out_specs=pl.BlockSpec((None, nt, 1024), lambda b, pt: (b, 0, 0)))
    return pl.pallas_call(
        functools.partial(_even_sample_body, lam_init=lam_init),
        out_shape=jax.ShapeDtypeStruct((nb, nt, 1024), F32), grid_spec=gs,
        compiler_params=_cparams(("arbitrary",)), name="even_sample")(
            page_table, qbd_a, qbd_b, dkv_new, fkv_new, lnew, lam_p,
            *([cache_d] * N_PAGES), *([cache_f] * N_PAGES), *([cache_l] * N_PAGES))


def _nsa_prompt_body(q_ref, gate_ref, kc_ref, kv_ref, oh_ref, o_ref, *, nblk):
    nq = q_ref.shape[1]
    s_len = kv_ref.shape[0]
    n_cmp = kc_ref.shape[0]
    n_sel = s_len // SEL_BLOCK
    p = pl.program_id(1)
    blocks = (p, nblk - 1 - p)
    nrow = D_REP * nq
    wq = D_HEADS * HD
    lo_q = lax.broadcasted_iota(jnp.int32, (nq, LANES), 1) < HALF
    row_q = lax.broadcasted_iota(jnp.int32, (nrow, 1), 0) % nq
    diag_mask = lax.broadcasted_iota(jnp.int32, (nrow, nq), 1) <= row_q
    pr_i = lax.broadcasted_iota(jnp.int32, (n_sel, n_cmp), 0)
    pc_i = lax.broadcasted_iota(jnp.int32, (n_sel, n_cmp), 1)
    pair_t = jnp.where(pc_i // (SEL_BLOCK // CMP_BLOCK) == pr_i, 1.0, 0.0)
    blk = lax.broadcasted_iota(jnp.int32, (n_sel, nq), 0)
    qlane = lax.broadcasted_iota(jnp.int32, (n_sel, nq), 1)
    wlen = WINDOW + nq

    def stack_q(i, base, g):
        parts = []
        for rep in range(D_REP):
            hd = g * D_REP + rep
            slab = q_ref[i, :, base + (hd // 2) * LANES:base + (hd // 2 + 1) * LANES]
            keep = lo_q if hd % 2 == 0 else jnp.logical_not(lo_q)
            parts.append(jnp.where(keep, slab, jnp.zeros_like(slab)))
        return jnp.concatenate(parts, axis=0)

    for g in range(D_KV):
        kc = kc_ref[:, g * LANES:(g + 1) * LANES].astype(BF16)
        vc = kc_ref[:, (2 + g) * LANES:(3 + g) * LANES].astype(BF16)
        q_aug, o_c, o_w = [], [], []
        for i in range(2):
            q0 = blocks[i] * nq
            qpos_col = q0 + row_q
            qc = stack_q(i, 0, g)
            qr = stack_q(i, wq, g)
            sc = lax.dot_general(qc, kc, NT_DIMS, preferred_element_type=F32)
            c_end = lax.broadcasted_iota(jnp.int32, (nrow, n_cmp), 1) * CMP_BLOCK + (CMP_BLOCK - 1)
            pc = _masked_softmax_rows(sc, c_end <= qpos_col)
            o_c.append(jnp.dot(pc.astype(BF16), vc, preferred_element_type=F32))
            imp_c = pc[0:nq] + pc[nq:2 * nq] + pc[2 * nq:3 * nq] + pc[3 * nq:4 * nq]
            imp_t = lax.dot_general(pair_t, imp_c, NT_DIMS, precision=HIGHEST, preferred_element_type=F32)
            qp = q0 + qlane
            score = jnp.where(qp // SEL_BLOCK == blk, D_REP + 1.0, jnp.where(blk * SEL_BLOCK <= qp, imp_t, -1.0))
            sel_t = _topk_rows(score, min(N_SEL, n_sel))
            pen_t = jnp.concatenate([(sel_t - 1.0) * BIG, jnp.zeros((LANES - n_sel, nq), F32)], axis=0)
            pen = pen_t.T.astype(BF16)
            q_aug.append(jnp.concatenate([qr, jnp.concatenate([pen] * D_REP, axis=0)], axis=1))
            ws = pl.multiple_of(jnp.maximum(q0 - WINDOW, 0), nq)
            wk = kv_ref[pl.ds(ws, wlen), (4 + g) * LANES:(5 + g) * LANES]
            wv = kv_ref[pl.ds(ws, wlen), (6 + g) * LANES:(7 + g) * LANES]
            wpos = ws + lax.broadcasted_iota(jnp.int32, (nrow, wlen), 1)
            wmask = (wpos <= qpos_col) & (wpos > qpos_col - WINDOW)
            pw = _masked_softmax_rows(lax.dot_general(qr, wk, NT_DIMS, preferred_element_type=F32), wmask)
            o_w.append(jnp.dot(pw.astype(BF16), wv, preferred_element_type=F32))

        def score_fn(qh, kb, g=g):
            ks = pl.multiple_of(kb * nq, nq)
            k_aug = jnp.concatenate([kv_ref[pl.ds(ks, nq), g * LANES:(g + 1) * LANES], oh_ref[pl.ds(ks, nq), :]],
                                    axis=1)
            return lax.dot_general(qh, k_aug, NT_DIMS, preferred_element_type=F32)

        def v_fn(kb, g=g):
            return kv_ref[pl.ds(pl.multiple_of(kb * nq, nq), nq), (2 + g) * LANES:(3 + g) * LANES]

        o_s = _paired_causal_softmax_pv(q_aug, score_fn, v_fn, p, nblk, diag_mask)

        for i in range(2):
            gates = gate_ref[i]
            heads = []
            for rep in range(D_REP):
                hd = g * D_REP + rep
                rows = slice(rep * nq, (rep + 1) * nq)
                heads.append(gates[:, 3 * hd:3 * hd + 1] * o_c[i][rows] + gates[:, 3 * hd + 1:3 * hd + 2] * o_s[i][rows]
                             + gates[:, 3 * hd + 2:3 * hd + 3] * o_w[i][rows])
            for j in range(D_REP // 2):
                slab = g * (D_REP // 2) + j
                o_ref[i, :, slab * LANES:(slab + 1) * LANES] = jnp.where(lo_q, heads[2 * j], heads[2 * j + 1])


def _nsa_prompt(q, gates, kcd, nsa_kv, onehot):
    b, s, _ = nsa_kv.shape
    nq = TOKEN_TILE
    nblk = s // nq
    q4 = q.reshape(b, 2, s // 2, q.shape[-1])
    g4 = gates.reshape(b, 2, s // 2, gates.shape[-1])
    out = pl.pallas_call(
        functools.partial(_nsa_prompt_body, nblk=nblk),
        out_shape=jax.ShapeDtypeStruct((b, 2, s // 2, D_HEADS * HD), F32), grid=(b, nblk // 2),
        in_specs=[pl.BlockSpec((None, 2, nq, q.shape[-1]), lambda bi, p: (bi, 0, p, 0)),
                  pl.BlockSpec((None, 2, nq, gates.shape[-1]), lambda bi, p: (bi, 0, p, 0)),
                  pl.BlockSpec((None,) + kcd.shape[1:], lambda bi, p: (bi, 0, 0)),
                  pl.BlockSpec((None, s, nsa_kv.shape[-1]), lambda bi, p: (bi, 0, 0)),
                  pl.BlockSpec(onehot.shape, lambda bi, p: (0, 0))],
        out_specs=pl.BlockSpec((None, 2, nq, D_HEADS * HD), lambda bi, p: (bi, 0, p, 0)),
        compiler_params=_cparams(("arbitrary", "arbitrary")), name="nsa_prompt")(q4, g4, kcd, nsa_kv, onehot)
    return out.reshape(b * s, D_HEADS * HD)


def _nsa_sample_body(pt_ref, qc_ref, qr_ref, gate_ref, snew_ref, wnew_ref, wst_ref, avg_ref, *rest):
    del pt_ref
    o_ref = rest[-1]
    for j in range(NSA_DECODE_ROWS):
        pages = rest[2 * N_PAGES * j:2 * N_PAGES * (j + 1)]
        _nsa_sample_row(qc_ref.at[j], qr_ref.at[j], gate_ref.at[j], snew_ref.at[j], wnew_ref.at[j], wst_ref.at[j],
                        avg_ref, pages[0:N_PAGES], pages[N_PAGES:2 * N_PAGES], o_ref.at[j])


def _nsa_sample_row(qc_ref, qr_ref, gate_ref, snew_ref, wnew_ref, wst_ref, avg_ref, cpages, spages, o_ref):
    nrow = qc_ref.shape[0]
    nt = snew_ref.shape[0]
    rid = lax.broadcasted_iota(jnp.int32, (nrow, 1), 0)
    t_row = rid // D_HEADS
    qc = qc_ref[...]
    qr = qr_ref[...]
    qrf = qr.astype(F32)

    cmp_t = jnp.concatenate([pg[...].astype(BF16) for pg in cpages], axis=1)
    means_t = jnp.dot(cmp_t, avg_ref[...], preferred_element_type=F32)
    kc_t = means_t[0:LANES].astype(BF16)
    vc_t = means_t[LANES:2 * LANES].astype(BF16)
    sc = jnp.dot(qc, kc_t, preferred_element_type=F32)
    mc = jnp.max(sc, axis=-1, keepdims=True)
    pc = jnp.exp2(sc - mc)
    pc = pc / jnp.maximum(jnp.sum(pc, axis=-1, keepdims=True), TINY)
    o_c = lax.dot_general(pc.astype(BF16), vc_t, NT_DIMS, preferred_element_type=F32)

    ncol = nt * D_KV
    gr = lax.broadcasted_iota(jnp.int32, (ncol, nrow), 0)
    gc = lax.broadcasted_iota(jnp.int32, (ncol, nrow), 1)
    gsum = jnp.where(gc // D_REP == gr, 1.0, 0.0)
    imp_c = jnp.dot(gsum, pc, precision=HIGHEST, preferred_element_type=F32)
    pr = lax.broadcasted_iota(jnp.int32, (N_PAST_SEL, N_PAST_CMP), 0)
    pcx = lax.broadcasted_iota(jnp.int32, (N_PAST_SEL, N_PAST_CMP), 1)
    pair_t = jnp.where(pcx // (SEL_BLOCK // CMP_BLOCK) == pr, 1.0, 0.0)
    imp_t = lax.dot_general(pair_t, imp_c, NT_DIMS, precision=HIGHEST, preferred_element_type=F32)
    sel_t = _topk_rows(imp_t, N_SEL - 1).astype(BF16)
    er = lax.broadcasted_iota(jnp.int32, (nrow, ncol), 0)
    ec = lax.broadcasted_iota(jnp.int32, (nrow, ncol), 1)
    g_t = jnp.where(er // D_REP == ec, 1.0, 0.0).astype(BF16)
    sel_r = lax.dot_general(g_t, sel_t, NT_DIMS, preferred_element_type=F32)
    eb_r = lax.broadcasted_iota(jnp.int32, (N_PAST_SEL, PAST_LEN), 0)
    eb_c = lax.broadcasted_iota(jnp.int32, (N_PAST_SEL, PAST_LEN), 1)
    eb = jnp.where(eb_c // SEL_BLOCK == eb_r, 1.0, 0.0).astype(BF16)
    smask = jnp.dot(sel_r.astype(BF16), eb, preferred_element_type=F32) > 0.5

    def new_rows(s_past, mask_past, new_ref):
        s_past = jnp.where(mask_past, s_past, NEG)
        s_new = [jnp.where(t_row >= j, jnp.sum(qrf * new_ref[j:j + 1, 0:LANES], axis=-1, keepdims=True), NEG)
                 for j in range(nt)]
        m = jnp.max(s_past, axis=-1, keepdims=True)
        for sj in s_new:
            m = jnp.maximum(m, sj)
        p = jnp.where(mask_past, jnp.exp2(s_past - m), 0.0)
        p_new = [jnp.where(t_row >= j, jnp.exp2(s_new[j] - m), 0.0) for j in range(nt)]
        l = jnp.sum(p, axis=-1, keepdims=True)
        for pj in p_new:
            l = l + pj
        inv = 1.0 / jnp.maximum(l, TINY)
        return p * inv, [pj * inv for pj in p_new]

    s_parts = [jnp.dot(qr, pg[0:LANES, :].astype(BF16), preferred_element_type=F32) for pg in spages]
    ps, ps_new = new_rows(jnp.concatenate(s_parts, axis=1), smask, snew_ref)
    psb = ps.astype(BF16)
    o_s = None
    for i, pg in enumerate(spages):
        part = lax.dot_general(psb[:, i * PAGE_SIZE:(i + 1) * PAGE_SIZE], pg[LANES:2 * LANES, :].astype(BF16),
                               NT_DIMS, preferred_element_type=F32)
        o_s = part if o_s is None else o_s + part
    for j in range(nt):
        o_s = o_s + ps_new[j] * snew_ref[j:j + 1, LANES:2 * LANES]

    wb = wst_ref.shape[1]
    wk_t = wst_ref[0:LANES, :].astype(BF16)
    wv_t = wst_ref[LANES:2 * LANES, :].astype(BF16)
    jpos = lax.broadcasted_iota(jnp.int32, (nrow, wb), 1)
    wmask = jpos > t_row - (WINDOW - wb)
    pw, pw_new = new_rows(jnp.dot(qr, wk_t, preferred_element_type=F32), wmask, wnew_ref)
    o_w = lax.dot_general(pw.astype(BF16), wv_t, NT_DIMS, preferred_element_type=F32)
    for j in range(nt):
        o_w = o_w + pw_new[j] * wnew_ref[j:j + 1, LANES:2 * LANES]

    g = gate_ref[...]
    o_ref[...] = g[:, 0:1] * o_c + g[:, 1:2] * o_s + g[:, 2:3] * o_w


def _nsa_sample_attn(o, page_table, qc_bd, qr_bd, gates, snew, wnew, win_state, cache_c, cache_s):
    nb = qc_bd.shape[0]
    rows = NSA_DECODE_ROWS
    kv2 = 2 * D_KV * HD
    key = jnp.arange(PAST_LEN)[:, None] // CMP_BLOCK
    avg = jnp.where(key == jnp.arange(N_PAST_CMP)[None, :], 1.0 / CMP_BLOCK, 0.0).astype(BF16)

    def per_b(shape):
        nd = len(shape)
        return pl.BlockSpec((rows,) + tuple(shape[1:]), lambda b, pt: (b,) + (0,) * (nd - 1))

    def page(j, p):
        return pl.BlockSpec((None, None, kv2, PAGE_SIZE), lambda b, pt: (o, pt[b * rows + j, p], 0, 0))

    in_specs = [per_b(qc_bd.shape), per_b(qr_bd.shape), per_b(gates.shape), per_b(snew.shape), per_b(wnew.shape),
                pl.BlockSpec((None, rows) + win_state.shape[2:], lambda b, pt: (o, b, 0, 0)),
                pl.BlockSpec(avg.shape, lambda b, pt: (0, 0))]
    page_args = []
    for j in range(rows):
        in_specs += [page(j, p) for p in range(N_PAGES)] + [page(j, p) for p in range(N_PAGES)]
        page_args += [cache_c] * N_PAGES + [cache_s] * N_PAGES
    gs = pltpu.PrefetchScalarGridSpec(
        num_scalar_prefetch=1, grid=(nb // rows,), in_specs=in_specs,
        out_specs=pl.BlockSpec((rows,) + qc_bd.shape[1:], lambda b, pt: (b, 0, 0)))
    return pl.pallas_call(
        _nsa_sample_body, out_shape=jax.ShapeDtypeStruct(qc_bd.shape, F32), grid_spec=gs,
        compiler_params=_cparams(("arbitrary",)), name="nsa_sample")(
            page_table, qc_bd, qr_bd, gates, snew, wnew, win_state, avg, *page_args)


def _rope_tables(pos):
    half = ROT_DIM // 2
    inv = 1.0 / (ROPE_THETA ** (jnp.arange(half, dtype=F32) * 2.0 / ROT_DIM))
    ang = pos.astype(F32)[:, None] * inv[None, :]
    cos, sin = jnp.cos(ang), jnp.sin(ang)
    n = pos.shape[0]
    zeros8 = jnp.zeros((n, half), F32)
    rest0 = jnp.zeros((n, HD - ROT_DIM), F32)
    c = jnp.concatenate([cos, cos, jnp.ones((n, HD - ROT_DIM), F32)], axis=1)
    s1 = jnp.concatenate([-sin, zeros8, rest0], axis=1)
    s2 = jnp.concatenate([zeros8, sin, rest0], axis=1)
    return tuple(jnp.concatenate([t, t], axis=1) for t in (c, s1, s2))


def _block_diag_rows(q, group):
    nb, nt, w = q.shape
    keep = (jnp.arange(w)[None, :] // HD) == jnp.arange(8)[:, None]
    return jnp.where(keep[None, None], q[:, :, None, :], jnp.zeros((), q.dtype)).reshape(nb, nt * 8, w)


def _nsa_rows(q):
    nb, nt, _ = q.shape
    qh = q.reshape(nb, nt, D_HEADS, HD)
    z = jnp.zeros_like(qh)
    first = (jnp.arange(D_HEADS) < D_REP)[None, None, :, None]
    out = jnp.concatenate([jnp.where(first, qh, z), jnp.where(first, z, qh)], axis=-1)
    return out.reshape(nb, nt * D_HEADS, 2 * HD)


def kernel(x_prompt, x_sample, cache_diff_kv, cache_fox_kv, cache_fox_logf, cache_nsa_cmp_kv, cache_nsa_sel_kv,
           state_nsa_win_kv, page_table, norm_g, ffn_w_in, ffn_w_out, even_w_in, even_w_out, fox_b_f, diff_lambda,
           diff_subln_g, odd_w_in, odd_w_out, gmlp_ln_g, gmlp_ln_b, gmlp_w_s, gmlp_b_s):
    bsz, seq, _ = x_prompt.shape
    nb, nt, _ = x_sample.shape
    tp = bsz * seq
    ts = nb * nt
    n_pool = cache_diff_kv.shape[1]

    yp = x_prompt.reshape(tp, D_MODEL)
    ys = x_sample.reshape(ts, D_MODEL)
    tab_p = _rope_tables(jnp.arange(seq))
    tab_s = _rope_tables(PAST_LEN + (jnp.arange(ts) % nt))
    win_bf = ffn_w_in.astype(BF16)
    wout_bf = ffn_w_out.astype(BF16)
    nblk_p = seq // TOKEN_TILE
    sel_onehot = (jnp.arange(seq)[:, None] // SEL_BLOCK == jnp.arange(LANES)[None, :]).astype(BF16)

    kv2 = 2 * D_KV * HD
    keys_last = (0, 1, 3, 4, 5, 2)
    cd = cache_diff_kv.reshape(cache_diff_kv.shape[0], n_pool, PAGE_SIZE * 2 * A_HEADS, 2 * HD)
    cf = cache_fox_kv.transpose(keys_last).reshape(cache_fox_kv.shape[0], n_pool, 2 * B_HEADS * HD, PAGE_SIZE)
    cl = cache_fox_logf.transpose(0, 1, 3, 2)
    cc = cache_nsa_cmp_kv.transpose(keys_last).reshape(cache_nsa_cmp_kv.shape[0], n_pool, kv2, PAGE_SIZE)
    cs = cache_nsa_sel_kv.transpose(keys_last).reshape(cache_nsa_sel_kv.shape[0], n_pool, kv2, PAGE_SIZE)
    wstate = state_nsa_win_kv.transpose(keys_last).reshape(state_nsa_win_kv.shape[0], nb, kv2,
                                                           state_nsa_win_kv.shape[2])

    w512 = 512
    outs = {k: [] for k in ("dkv_s", "fkv_s", "fl_p", "fl_s", "ckv_s", "skv_s", "wkv_s", "gv_s")}
    even_stack = odd_stack = None
    for i in range(DEPTH):
        nrm = norm_g[i]
        if i % 2 == 0:
            e = i // 2
            lam_init = 0.8 - 0.6 * math.exp(-0.3 * i)
            w = even_w_in[e]
            aq, ak, av, bq, bk, bv, bfc = jnp.split(w, [512, 1024, 1536, 2048, 2560, 3072], axis=1)
            wp = jnp.concatenate([aq, bq, ak, av, bk, bv, bfc,
                                  jnp.zeros((D_MODEL, EVEN_COLS_PAD - w.shape[1]), F32)], axis=1).astype(BF16)
            bf_pad = jnp.concatenate([fox_b_f[e], jnp.zeros((LANES - B_HEADS,), F32)])[None, :]
            lam_p = diff_lambda[e]
            sg = diff_subln_g[e][None, :]
            wmo = even_w_out[e].astype(BF16)

            hp, q, dkv, fkv, kb, vb, logf = _even_pre(yp, nrm, win_bf, wout_bf, i, wp, bf_pad, tab_p, nblk=nblk_p,
                                                      prev=even_stack)
            even_stack = (dkv, fkv)
            k3, v3 = (a.reshape(bsz, seq, 1024) for a in (kb, vb))
            oa = _pair_attn_prompt(q, k3, v3, lam_p, fox=False, lam_init=lam_init)
            logf_t = logf.reshape(bsz, seq, B_HEADS).transpose(0, 2, 1).reshape(bsz * B_HEADS, seq)
            frow = _cumsum_rows_of_lanes(logf_t).reshape(bsz, B_HEADS // 2, 2, seq)
            ob = _pair_attn_prompt(q, k3, v3, frow, fox=True)
            yp = _even_post(oa, ob, sg, wmo, hp, nrm, win_bf, wout_bf, i, 1.0 - lam_init, nblk=nblk_p)
            outs["fl_p"].append(logf.reshape(bsz, seq, B_HEADS))

            hs, q, dkv, fkv, kb, vb, logf = _even_pre(ys, nrm, win_bf, wout_bf, i, wp, bf_pad, tab_s)
            qs3 = q.reshape(nb, nt, 1024)
            o = _even_sample_attn(e, page_table, _block_diag_rows(qs3[:, :, :w512], 8),
                                  _block_diag_rows(qs3[:, :, w512:], 8), dkv.reshape(nb, nt, 1024),
                                  fkv.reshape(nb, nt, 1024), logf.reshape(nb, nt, B_HEADS), lam_p, cd, cf,
                                  cl, lam_init).reshape(ts, 1024)
            ys = _even_post(o[:, :w512], o[:, w512:], sg, wmo, hs, nrm, win_bf, wout_bf, i, 1.0 - lam_init)
            outs["dkv_s"].append(dkv.reshape(nb, nt, 2, A_HEADS, 2 * HD))
            outs["fkv_s"].append(fkv.reshape(nb, nt, 2, B_HEADS, HD))
            outs["fl_s"].append(logf.reshape(nb, nt, B_HEADS))
        else:
            o_idx = i // 2
            w = odd_w_in[o_idx]
            wp = jnp.concatenate([w, jnp.zeros((D_MODEL, ODD_COLS_PAD - w.shape[1]), F32)], axis=1).astype(BF16)
            lng = gmlp_ln_g[o_idx][None, :]
            lnb = gmlp_ln_b[o_idx][None, :]
            wmo = odd_w_out[o_idx].astype(BF16)
            ws = gmlp_w_s[o_idx]
            bs = gmlp_b_s[o_idx]
            tm = TOKEN_TILE
            kv_shape = (2, D_KV, HD)

            hp, u, v, q, kcd, ckv, skv, wkv, nsa_kv, gates = _odd_pre(yp, nrm, win_bf, wout_bf, i, wp, lng, lnb, tab_p,
                                                                      nblk=nblk_p, prev=odd_stack)
            odd_stack = (ckv, skv, wkv)
            od = _nsa_prompt(q, gates, kcd.reshape(bsz, seq // CMP_BLOCK, 4 * LANES),
                             nsa_kv.reshape(bsz, seq, 1024), sel_onehot)
            bs_p = jnp.tile(jnp.repeat(bs.T, C_GW, axis=1), (tm // CHUNK, 1))
            yp = _odd_post(u, v, od, ws, bs_p, wmo, hp, nrm, win_bf, wout_bf, i, True, nblk=nblk_p)

            hs, u, v, q, ckv, skv, wkv, gates = _odd_pre(ys, nrm, win_bf, wout_bf, i, wp, lng, lnb, tab_s)
            q3 = q.reshape(nb, nt, 1024)
            kv2 = 2 * D_KV * HD
            od_raw = _nsa_sample_attn(o_idx, page_table, _nsa_rows(q3[:, :, :512]), _nsa_rows(q3[:, :, 512:]),
                                      gates.reshape(nb, nt * D_HEADS, 3), skv.reshape(nb, nt, kv2),
                                      wkv.reshape(nb, nt, kv2), wstate, cc, cs)
            od4 = od_raw.reshape(nb, nt, D_HEADS, 2, HD)
            first = (jnp.arange(D_HEADS) < D_REP)[None, None, :, None]
            od = jnp.where(first, od4[:, :, :, 0], od4[:, :, :, 1]).reshape(ts, 512)
            tpos = jnp.arange(nt)
            coefs = []
            for k in range(nt):
                val = jnp.where((tpos >= k)[None, :], ws[:, tpos, jnp.maximum(tpos - k, 0)], 0.0)
                coefs.append(jnp.tile(jnp.repeat(val.T, C_GW, axis=1), (tm // nt, 1)))
            bs_s = jnp.tile(jnp.repeat(bs[:, :nt].T, C_GW, axis=1), (tm // nt, 1))
            ys = _odd_post(u, v, od, jnp.stack(coefs), bs_s, wmo, hs, nrm, win_bf, wout_bf, i, False)
            outs["ckv_s"].append(ckv.reshape((nb, nt) + kv_shape))
            outs["skv_s"].append(skv.reshape((nb, nt) + kv_shape))
            outs["wkv_s"].append(wkv.reshape((nb, nt) + kv_shape))
            outs["gv_s"].append(v.reshape(nb, nt, C_CH))

    st = lambda k: jnp.stack(outs[k], axis=0)
    n_even, n_odd = (DEPTH + 1) // 2, DEPTH // 2
    tokens_first = (0, 1, 5, 2, 3, 4)
    dkv_p = even_stack[0].reshape(n_even, bsz, seq, 2, A_HEADS, 2 * HD)
    fkv_p = even_stack[1].reshape(n_even, bsz, 2, B_HEADS, HD, seq).transpose(tokens_first)
    win_rows = min(WINDOW, seq)
    ckv_p, skv_p, wkv_p = (a.reshape(n_odd, bsz, 2, D_KV, HD, seq).transpose(tokens_first) for a in odd_stack)
    return (yp.reshape(bsz, seq, D_MODEL), ys.reshape(nb, nt, D_MODEL), dkv_p, st("dkv_s"), fkv_p,
            st("fkv_s"), st("fl_p"), st("fl_s"), ckv_p, st("ckv_s"), skv_p, st("skv_s"), wkv_p[:, :, seq - win_rows:],
            st("wkv_s"), st("gv_s"))
```

```python
import functools
import math

import jax
import jax.numpy as jnp
from jax import lax
from jax.experimental import pallas as pl
from jax.experimental.pallas import tpu as pltpu

F32 = jnp.float32
BF16 = jnp.bfloat16

D_MODEL = 1024
DEPTH = 4
HD = 64
ROT_DIM = HD // 4
ROPE_THETA = 500000.0
A_HEADS = 4
B_HEADS = 8
C_CH = 512
C_GROUPS = 4
C_GW = C_CH // C_GROUPS
CHUNK = 128
D_HEADS = 8
D_KV = 2
D_REP = D_HEADS // D_KV
CMP_BLOCK = 32
SEL_BLOCK = 64
N_SEL = 8
WINDOW = 512
D_FF = 2816
EPS = 1e-6
NEG = -1e30
TINY = 1e-30
PAGE_SIZE = 128
PAST_LEN = 2048
N_PAGES = PAST_LEN // PAGE_SIZE
N_PAST_SEL = PAST_LEN // SEL_BLOCK
N_PAST_CMP = PAST_LEN // CMP_BLOCK
LOG2E = 1.4426950408889634
QSCALE = HD ** -0.5 * LOG2E
BIG = 1e30

LANES = 128
HALF = LANES // 2
VMEM_LIMIT = 56 * 2 ** 20
TOKEN_TILE = 256
EVEN_COLS_PAD = 3200
ODD_COLS_PAD = 2432
NSA_DECODE_ROWS = 2
NT_DIMS = (((1,), (1,)), ((), ()))
HIGHEST = lax.Precision.HIGHEST


def _cparams(sem=None):
    return pltpu.CompilerParams(dimension_semantics=sem, vmem_limit_bytes=VMEM_LIMIT)


def _resident(shape):
    nd = len(shape)
    return pl.BlockSpec(shape, lambda *_: (0,) * nd, pipeline_mode=pl.Buffered(1))


def _rms(x, g):
    return x * lax.rsqrt(jnp.mean(x * x, axis=-1, keepdims=True) + EPS) * g


def _lane_tile(t, width):
    return jnp.concatenate([t] * (width // LANES), axis=1)


def _rope(x, c, s1, s2):
    w = x.shape[1]
    return (x * _lane_tile(c, w) + pltpu.roll(x, w - ROT_DIM // 2, 1) * _lane_tile(s1, w)
            + pltpu.roll(x, ROT_DIM // 2, 1) * _lane_tile(s2, w))


def _dup_halves(x):
    r = pltpu.roll(x, HALF, 1)
    lo = lax.broadcasted_iota(jnp.int32, x.shape, 1) < HALF
    return jnp.where(lo, x, r), jnp.where(lo, r, x)


def _ffn_half(x, g_pre, g_post, win_ref, wout_ref):
    xn = _rms(x, g_pre).astype(BF16)
    gu = jnp.dot(xn, win_ref[...], preferred_element_type=F32)
    act = (jax.nn.silu(gu[:, 0:D_FF]) * gu[:, D_FF:2 * D_FF]).astype(BF16)
    acc = jnp.dot(act, wout_ref[...], preferred_element_type=F32)
    return x + 0.5 * _rms(acc, g_post)


def _cumsum_lanes(x):
    n = x.shape[1]
    lane = lax.broadcasted_iota(jnp.int32, x.shape, 1)
    s = 1
    while s < n:
        x = x + jnp.where(lane >= s, pltpu.roll(x, s, 1), 0.0)
        s *= 2
    return x


def _topk_rows(score, k):
    n = score.shape[0]
    row = lax.broadcasted_iota(jnp.int32, score.shape, 0).astype(F32)
    sel = jnp.zeros(score.shape, F32)
    for _ in range(k):
        m = jnp.max(score, axis=0, keepdims=True)
        first = jnp.min(jnp.where(score == m, row, float(n)), axis=0, keepdims=True)
        pick = row == first
        sel = jnp.where(pick, 1.0, sel)
        score = jnp.where(pick, -2.0, score)
    return sel


def _masked_softmax_rows(s, mask):
    s = jnp.where(mask, s, NEG)
    m = jnp.max(s, axis=-1, keepdims=True)
    p = jnp.where(mask, jnp.exp2(s - m), 0.0)
    return p / jnp.maximum(jnp.sum(p, axis=-1, keepdims=True), TINY)


def _even_pre_body(*refs, final_layout):
    x_ref, nrm_ref, win_ref, wout_ref, wp_ref, bf_ref, c_ref, s1_ref, s2_ref = refs[:9]
    h_ref, q_ref, dkv_ref, fkv_ref, k_ref, v_ref, logf_ref = refs[-7:]
    tm = x_ref.shape[0]
    h = _ffn_half(x_ref[...], nrm_ref[0:1, :], nrm_ref[1:2, :], win_ref, wout_ref)
    h_ref[...] = h
    hn = _rms(h, nrm_ref[2:3, :]).astype(BF16)
    z = jnp.dot(hn, wp_ref[...], preferred_element_type=F32)
    c, s1, s2 = c_ref[...], s1_ref[...], s2_ref[...]
    w = A_HEADS * 2 * HD
    aq = _rope(z[:, 0:w], c, s1, s2)
    bq = z[:, w:2 * w]
    ak = _rope(z[:, 2 * w:3 * w], c, s1, s2)
    av = z[:, 3 * w:4 * w]
    bk = z[:, 4 * w:5 * w]
    bv = z[:, 5 * w:6 * w]
    q_ref[:, 0:w] = (aq * QSCALE).astype(BF16)
    q_ref[:, w:2 * w] = (bq * QSCALE).astype(BF16)
    if final_layout:
        for hd in range(A_HEADS):
            dkv_ref[pl.ds(hd, tm, stride=2 * A_HEADS), :] = ak[:, hd * 2 * HD:(hd + 1) * 2 * HD]
            dkv_ref[pl.ds(A_HEADS + hd, tm, stride=2 * A_HEADS), :] = av[:, hd * 2 * HD:(hd + 1) * 2 * HD]
        fkv_ref[0:w, :] = bk.T
        fkv_ref[w:2 * w, :] = bv.T
    else:
        dkv_ref[:, 0:w] = ak
        dkv_ref[:, w:2 * w] = av
        fkv_ref[:, 0:w] = bk
        fkv_ref[:, w:2 * w] = bv
    k_ref[:, 0:w] = ak.astype(BF16)
    k_ref[:, w:2 * w] = bk.astype(BF16)
    v_ref[:, 0:w] = av.astype(BF16)
    v_ref[:, w:2 * w] = bv.astype(BF16)
    lf = jax.nn.log_sigmoid(z[:, 6 * w:6 * w + LANES] + bf_ref[...])
    logf_ref[...] = lf[:, 0:B_HEADS]


def _odd_pre_body(*refs, final_layout):
    x_ref, nrm_ref, win_ref, wout_ref, wp_ref, lng_ref, lnb_ref, c_ref, s1_ref, s2_ref = refs[:10]
    if final_layout:
        h_ref, u_ref, v_ref, q_ref, kcd_ref, cmp_ref, sel_ref, win_kv_ref, nsa_ref, gate_ref = refs[-10:]
    else:
        h_ref, u_ref, v_ref, q_ref, cmp_ref, sel_ref, win_kv_ref, gate_ref = refs[-8:]
    h = _ffn_half(x_ref[...], nrm_ref[0:1, :], nrm_ref[1:2, :], win_ref, wout_ref)
    h_ref[...] = h
    hn = _rms(h, nrm_ref[2:3, :]).astype(BF16)
    z = jnp.dot(hn, wp_ref[...], preferred_element_type=F32)
    c, s1, s2 = c_ref[...], s1_ref[...], s2_ref[...]
    u_ref[...] = jax.nn.gelu(z[:, 0:C_CH])
    gv = jax.nn.gelu(z[:, C_CH:2 * C_CH])
    for g in range(C_GROUPS):
        blk = gv[:, g * C_GW:(g + 1) * C_GW]
        mu = jnp.mean(blk, axis=-1, keepdims=True)
        var = jnp.mean(jnp.square(blk - mu), axis=-1, keepdims=True)
        v_ref[:, g * C_GW:(g + 1) * C_GW] = ((blk - mu) * lax.rsqrt(var + EPS) * lng_ref[:, g * C_GW:(g + 1) * C_GW]
                                             + lnb_ref[:, g * C_GW:(g + 1) * C_GW])
    o = 2 * C_CH
    wq = D_HEADS * HD
    q = z[:, o:o + wq]
    q_ref[:, 0:wq] = (q * QSCALE).astype(BF16)
    q_ref[:, wq:2 * wq] = (_rope(q, c, s1, s2) * QSCALE).astype(BF16)
    o += wq
    kvw = D_KV * HD
    ck, cv = z[:, o:o + kvw], z[:, o + kvw:o + 2 * kvw]
    sk, sv = _rope(z[:, o + 2 * kvw:o + 3 * kvw], c, s1, s2), z[:, o + 3 * kvw:o + 4 * kvw]
    wk, wv = _rope(z[:, o + 4 * kvw:o + 5 * kvw], c, s1, s2), z[:, o + 5 * kvw:o + 6 * kvw]
    if final_layout:
        tm = x_ref.shape[0]
        ckv = jnp.concatenate([ck, cv], axis=1)
        cmp_ref[...] = ckv.T
        sel_ref[...] = jnp.concatenate([sk, sv], axis=1).T
        win_kv_ref[...] = jnp.concatenate([wk, wv], axis=1).T
        means = jnp.sum(ckv.reshape(tm // CMP_BLOCK, CMP_BLOCK, 2 * kvw), axis=1) * (1.0 / CMP_BLOCK)
        for j in range(2):
            d0, d1 = _dup_halves(means[:, j * LANES:(j + 1) * LANES])
            kcd_ref[:, (2 * j) * LANES:(2 * j + 1) * LANES] = d0
            kcd_ref[:, (2 * j + 1) * LANES:(2 * j + 2) * LANES] = d1
        for j, arr in enumerate((sk, sv, wk, wv)):
            d0, d1 = _dup_halves(arr)
            nsa_ref[:, (2 * j) * LANES:(2 * j + 1) * LANES] = d0.astype(BF16)
            nsa_ref[:, (2 * j + 1) * LANES:(2 * j + 2) * LANES] = d1.astype(BF16)
    else:
        cmp_ref[:, 0:kvw] = ck
        cmp_ref[:, kvw:2 * kvw] = cv
        sel_ref[:, 0:kvw] = sk
        sel_ref[:, kvw:2 * kvw] = sv
        win_kv_ref[:, 0:kvw] = wk
        win_kv_ref[:, kvw:2 * kvw] = wv
    o += 6 * kvw
    gate_ref[...] = jax.nn.sigmoid(z[:, o:o + LANES])[:, 0:3 * D_HEADS]


def _mixer_post(o, wmo_ref, h, nrm_ref, win_ref, wout_ref):
    m = jnp.dot(o.astype(BF16), wmo_ref[...], preferred_element_type=F32)
    h2 = h + _rms(m, nrm_ref[3:4, :])
    return _ffn_half(h2, nrm_ref[4:5, :], nrm_ref[5:6, :], win_ref, wout_ref)


def _even_post_body(oa_ref, ob_ref, sg_ref, wmo_ref, h_ref, nrm_ref, win_ref, wout_ref, y_ref, *, out_scale):
    sg = sg_ref[...]
    parts = []
    for hd in range(A_HEADS):
        blk = oa_ref[:, hd * 2 * HD:(hd + 1) * 2 * HD]
        parts.append(_rms(blk, sg) * out_scale)
    parts.append(ob_ref[...])
    o = jnp.concatenate(parts, axis=1)
    y_ref[...] = _mixer_post(o, wmo_ref, h_ref[...], nrm_ref, win_ref, wout_ref)


def _odd_post_body(u_ref, v_ref, od_ref, ws_ref, bs_ref, wmo_ref, h_ref, nrm_ref, win_ref, wout_ref, y_ref,
                   *, chunked):
    tm = u_ref.shape[0]
    parts = []
    if chunked:
        keep = (lax.broadcasted_iota(jnp.int32, (CHUNK, CHUNK), 1) <= lax.broadcasted_iota(jnp.int32, (CHUNK, CHUNK), 0))
        for g in range(C_GROUPS):
            wm = jnp.where(keep, ws_ref[g], 0.0).astype(BF16)
            vg = v_ref[:, g * C_GW:(g + 1) * C_GW].astype(BF16)
            mix = jnp.concatenate([jnp.dot(wm, vg[ch * CHUNK:(ch + 1) * CHUNK], preferred_element_type=F32)
                                   for ch in range(tm // CHUNK)], axis=0) + bs_ref[:, g * C_GW:(g + 1) * C_GW]
            parts.append(u_ref[:, g * C_GW:(g + 1) * C_GW] * mix)
    else:
        v = v_ref[...]
        mix = ws_ref[0] * v + bs_ref[...]
        for k in range(1, ws_ref.shape[0]):
            mix = mix + ws_ref[k] * pltpu.roll(v, k, 0)
        parts.append(u_ref[...] * mix)
    parts.append(od_ref[...])
    o = jnp.concatenate(parts, axis=1)
    y_ref[...] = _mixer_post(o, wmo_ref, h_ref[...], nrm_ref, win_ref, wout_ref)


def _row_spec(tm, width):
    return pl.BlockSpec((tm, width), lambda i: (i, 0))


def _pair_slot(i, nblk):
    half = nblk // 2
    qi = i % nblk
    return (i // nblk) * nblk + jnp.where(qi < half, qi, half + nblk - 1 - qi)


def _slot_spec(tm, width, nblk):
    if nblk is None:
        return _row_spec(tm, width)
    return pl.BlockSpec((tm, width), lambda i: (_pair_slot(i, nblk), 0))


def _table_spec(tm, rows):
    nrep = rows // tm
    return pl.BlockSpec((tm, LANES), lambda i: (i % nrep, 0))


def _ffn_specs(win_all, wout_all, layer, half):
    def pick(shape):
        return pl.BlockSpec((None, None) + tuple(shape[2:]), lambda *_: (layer, half, 0, 0),
                            pipeline_mode=pl.Buffered(1))
    return [pick(win_all.shape), pick(wout_all.shape)]


def _stacked(layer, nlayers, prev, shape, block, index):
    spec = pl.BlockSpec((None,) + block, lambda i: (layer,) + index(i))
    return jax.ShapeDtypeStruct((nlayers,) + shape, F32), spec, prev


def _even_pre(x, nrm, win_all, wout_all, layer, wp, bf_pad, tables, nblk=None, prev=None):
    t = x.shape[0]
    tm = TOKEN_TILE
    tabs = [_table_spec(tm, tables[0].shape[0])] * 3
    final = nblk is not None
    sds = jax.ShapeDtypeStruct
    aliases, extra_in, extra_specs = {}, [], []
    if final:
        n_even, e, bsz = (DEPTH + 1) // 2, layer // 2, t // (nblk * tm)
        dkv = _stacked(e, n_even, None if prev is None else prev[0], (t * 2 * A_HEADS, 2 * HD),
                       (tm * 2 * A_HEADS, 2 * HD), lambda i: (i, 0))
        fkv = _stacked(e, n_even, None if prev is None else prev[1], (bsz, 2 * B_HEADS * HD, nblk * tm),
                       (None, 2 * B_HEADS * HD, tm), lambda i: (i // nblk, 0, i % nblk))
        for out_idx, st in ((2, dkv), (3, fkv)):
            if st[2] is not None:
                aliases[9 + len(extra_in)] = out_idx
                extra_in.append(st[2])
                extra_specs.append(pl.BlockSpec(memory_space=pl.ANY))
        kv_shapes, kv_specs = (dkv[0], fkv[0]), (dkv[1], fkv[1])
    else:
        kv_shapes = (sds((t, 1024), F32), sds((t, 1024), F32))
        kv_specs = (_row_spec(tm, 1024), _row_spec(tm, 1024))
    outs = (sds((t, D_MODEL), F32), sds((t, 1024), BF16)) + kv_shapes + (sds((t, 1024), BF16), sds((t, 1024), BF16),
                                                                        sds((t, B_HEADS), F32))
    return pl.pallas_call(
        functools.partial(_even_pre_body, final_layout=final), out_shape=outs, grid=(t // tm,),
        in_specs=[_row_spec(tm, D_MODEL), _resident(nrm.shape)] + _ffn_specs(win_all, wout_all, layer, 0)
        + [_resident(wp.shape), _resident(bf_pad.shape)] + tabs + extra_specs,
        out_specs=(_row_spec(tm, D_MODEL), _slot_spec(tm, 1024, nblk)) + kv_specs
        + (_row_spec(tm, 1024), _row_spec(tm, 1024), _row_spec(tm, B_HEADS)),
        input_output_aliases=aliases,
        compiler_params=_cparams(("arbitrary",)), name="even_pre")(x, nrm, win_all, wout_all, wp, bf_pad, *tables,
                                                                   *extra_in)


def _odd_pre(x, nrm, win_all, wout_all, layer, wp, lng, lnb, tables, nblk=None, prev=None):
    t = x.shape[0]
    tm = TOKEN_TILE
    tabs = [_table_spec(tm, tables[0].shape[0])] * 3
    kv2 = 2 * D_KV * HD
    final = nblk is not None
    sds = jax.ShapeDtypeStruct
    aliases, extra_in, extra_specs = {}, [], []
    head = (sds((t, D_MODEL), F32), sds((t, C_CH), F32), sds((t, C_CH), F32), sds((t, 1024), BF16))
    head_specs = (_row_spec(tm, D_MODEL), _row_spec(tm, C_CH), _row_spec(tm, C_CH), _slot_spec(tm, 1024, nblk))
    gate = (sds((t, 3 * D_HEADS), F32),)
    gate_spec = (_slot_spec(tm, 3 * D_HEADS, nblk),)
    if final:
        n_odd, o, bsz = DEPTH // 2, layer // 2, t // (nblk * tm)
        stacks = [_stacked(o, n_odd, None if prev is None else prev[j], (bsz, kv2, nblk * tm), (None, kv2, tm),
                           lambda i: (i // nblk, 0, i % nblk)) for j in range(3)]
        for j, st in enumerate(stacks):
            if st[2] is not None:
                aliases[10 + len(extra_in)] = 5 + j
                extra_in.append(st[2])
                extra_specs.append(pl.BlockSpec(memory_space=pl.ANY))
        outs = (head + (sds((t // CMP_BLOCK, 4 * LANES), F32),) + tuple(st[0] for st in stacks)
                + (sds((t, 1024), BF16),) + gate)
        out_specs = (head_specs + (_row_spec(tm // CMP_BLOCK, 4 * LANES),) + tuple(st[1] for st in stacks)
                     + (_row_spec(tm, 1024),) + gate_spec)
    else:
        outs = head + (sds((t, kv2), F32),) * 3 + gate
        out_specs = head_specs + (_row_spec(tm, kv2),) * 3 + gate_spec
    return pl.pallas_call(
        functools.partial(_odd_pre_body, final_layout=final), out_shape=outs, grid=(t // tm,),
        in_specs=[_row_spec(tm, D_MODEL), _resident(nrm.shape)] + _ffn_specs(win_all, wout_all, layer, 0)
        + [_resident(wp.shape), _resident(lng.shape), _resident(lnb.shape)] + tabs + extra_specs,
        out_specs=out_specs, input_output_aliases=aliases,
        compiler_params=_cparams(("arbitrary",)), name="odd_pre")(x, nrm, win_all, wout_all, wp, lng, lnb, *tables,
                                                                  *extra_in)


def _even_post(oa, ob, sg, wmo, h, nrm, win_all, wout_all, layer, out_scale, nblk=None):
    t = h.shape[0]
    tm = TOKEN_TILE
    return pl.pallas_call(
        functools.partial(_even_post_body, out_scale=out_scale),
        out_shape=jax.ShapeDtypeStruct((t, D_MODEL), F32), grid=(t // tm,),
        in_specs=[_slot_spec(tm, 512, nblk), _slot_spec(tm, 512, nblk), _resident(sg.shape), _resident(wmo.shape),
                  _row_spec(tm, D_MODEL), _resident(nrm.shape)] + _ffn_specs(win_all, wout_all, layer, 1),
        out_specs=_row_spec(tm, D_MODEL),
        compiler_params=_cparams(("arbitrary",)), name="even_post")(oa, ob, sg, wmo, h, nrm, win_all, wout_all)


def _odd_post(u, v, od, ws, bs_full, wmo, h, nrm, win_all, wout_all, layer, chunked, nblk=None):
    t = h.shape[0]
    tm = TOKEN_TILE
    return pl.pallas_call(
        functools.partial(_odd_post_body, chunked=chunked),
        out_shape=jax.ShapeDtypeStruct((t, D_MODEL), F32), grid=(t // tm,),
        in_specs=[_row_spec(tm, C_CH), _row_spec(tm, C_CH), _slot_spec(tm, 512, nblk), _resident(ws.shape),
                  _resident(bs_full.shape), _resident(wmo.shape), _row_spec(tm, D_MODEL), _resident(nrm.shape)]
        + _ffn_specs(win_all, wout_all, layer, 1),
        out_specs=_row_spec(tm, D_MODEL),
        compiler_params=_cparams(("arbitrary",)), name="odd_post")(u, v, od, ws, bs_full, wmo, h, nrm, win_all,
                                                                   wout_all)


def _lambda_value(lam_ref, lam_init):
    lp = lam_ref[...]
    a = jnp.sum(lp[0:1, :] * lp[1:2, :], axis=-1, keepdims=True)
    b = jnp.sum(lp[2:3, :] * lp[3:4, :], axis=-1, keepdims=True)
    return jnp.exp(a) - jnp.exp(b) + lam_init


def _cumsum_body(x_ref, o_ref):
    o_ref[...] = _cumsum_lanes(x_ref[...]) * LOG2E


def _fold(x, op):
    out = x[:, 0:LANES]
    for t in range(1, x.shape[1] // LANES):
        out = op(out, x[:, t * LANES:(t + 1) * LANES])
    return out


def _paired_causal_softmax_pv(qpair, score_fn, v_fn, p, nblk, diag_mask):
    blocks = (p, nblk - 1 - p)
    steps = []
    for u in range(nblk - 1):
        is_a = u < p
        steps.append((is_a, jnp.where(is_a, u, u - p)))
    sd = [jnp.where(diag_mask, score_fn(qpair[i], blocks[i]), NEG) for i in range(2)]
    mrun = [_fold(sd[i], jnp.maximum) for i in range(2)]
    for is_a, kb in steps:
        f = _fold(score_fn(jnp.where(is_a, qpair[0], qpair[1]), kb), jnp.maximum)
        mrun[0] = jnp.where(is_a, jnp.maximum(mrun[0], f), mrun[0])
        mrun[1] = jnp.where(is_a, mrun[1], jnp.maximum(mrun[1], f))
    m = [jnp.max(mrun[i], axis=-1, keepdims=True) for i in range(2)]
    lrun, acc = [], []
    for i in range(2):
        pr = jnp.exp2(sd[i] - m[i])
        acc.append(jnp.dot(pr.astype(BF16), v_fn(blocks[i]), preferred_element_type=F32))
        lrun.append(_fold(pr, jnp.add))
    for is_a, kb in steps:
        s = score_fn(jnp.where(is_a, qpair[0], qpair[1]), kb)
        pr = jnp.exp2(s - jnp.where(is_a, m[0], m[1]))
        pv = jnp.dot(pr.astype(BF16), v_fn(kb), preferred_element_type=F32)
        f = _fold(pr, jnp.add)
        lrun[0] = jnp.where(is_a, lrun[0] + f, lrun[0])
        lrun[1] = jnp.where(is_a, lrun[1], lrun[1] + f)
        acc[0] = jnp.where(is_a, acc[0] + pv, acc[0])
        acc[1] = jnp.where(is_a, acc[1], acc[1] + pv)
    return [acc[i] / jnp.maximum(jnp.sum(lrun[i], axis=-1, keepdims=True), TINY) for i in range(2)]


def _cumsum_rows_of_lanes(x):
    return pl.pallas_call(_cumsum_body, out_shape=jax.ShapeDtypeStruct(x.shape, F32), name="fox_cumsum",
                          compiler_params=_cparams())(x)


def _pair_attn_body(*refs, fox, lam_init, nblk):
    q_ref, k_ref, v_ref, extra_ref, o_ref = refs
    tq = q_ref.shape[1]
    p = pl.program_id(2)
    lo = lax.broadcasted_iota(jnp.int32, (tq, LANES), 1) < HALF
    diag_mask = (lax.broadcasted_iota(jnp.int32, (tq, tq), 1) <= lax.broadcasted_iota(jnp.int32, (tq, tq), 0))

    def v_fn(kb):
        return v_ref[pl.ds(pl.multiple_of(kb * tq, tq), tq), :]

    outs = []
    for c in range(2):
        def score_fn(qh, kb, c=c):
            ks = pl.multiple_of(kb * tq, tq)
            s = lax.dot_general(qh, k_ref[pl.ds(ks, tq), :], NT_DIMS, preferred_element_type=F32)
            if fox:
                s = s - extra_ref[c:c + 1, pl.ds(ks, tq)]
            return s

        qpair = []
        for i in range(2):
            x = q_ref[i]
            zero = jnp.zeros_like(x)
            qpair.append(jnp.where(lo, x, zero) if c == 0 else jnp.where(lo, zero, x))
        outs.append(_paired_causal_softmax_pv(qpair, score_fn, v_fn, p, nblk, diag_mask))
    for i in range(2):
        if fox:
            o_ref[i] = jnp.where(lo, outs[0][i], outs[1][i])
        else:
            o_ref[i] = outs[0][i] - _lambda_value(extra_ref, lam_init) * outs[1][i]


def _pair_attn_prompt(q, k, v, extra, *, fox, lam_init=0.0):
    b, s, _ = k.shape
    tq = TOKEN_TILE
    nblk = s // tq
    off = 4 if fox else 0
    q4 = q.reshape(b, 2, s // 2, q.shape[-1])
    in_specs = [pl.BlockSpec((None, 2, tq, LANES), lambda bi, j, p: (bi, 0, p, j + off)),
                pl.BlockSpec((None, s, LANES), lambda bi, j, p: (bi, 0, j + off)),
                pl.BlockSpec((None, s, LANES), lambda bi, j, p: (bi, 0, j + off))]
    if fox:
        in_specs += [pl.BlockSpec((None, None, 2, s), lambda bi, j, p: (bi, j, 0, 0))]
    else:
        in_specs += [pl.BlockSpec(extra.shape, lambda bi, j, p: (0, 0))]
    out = pl.pallas_call(
        functools.partial(_pair_attn_body, fox=fox, lam_init=lam_init, nblk=nblk),
        out_shape=jax.ShapeDtypeStruct((b, 2, s // 2, 512), F32), grid=(b, 4, nblk // 2),
        in_specs=in_specs,
        out_specs=pl.BlockSpec((None, 2, tq, LANES), lambda bi, j, p: (bi, 0, p, j)),
        compiler_params=_cparams(("arbitrary", "arbitrary", "arbitrary")),
        name="fox_prompt" if fox else "diff_prompt")(q4, k, v, extra)
    return out.reshape(b * s, 512)


def _paged_attend(q_ref, pages, score_page, pv_page, new_ref, bias_past, bias_new, t_row):
    w = q_ref.shape[1]
    qb = q_ref[...]
    s = jnp.concatenate([score_page(qb, pg) for pg in pages], axis=1)
    if bias_past is not None:
        s = s + bias_past
    qf = qb.astype(F32)
    nt = new_ref.shape[0]
    s_new = []
    for j in range(nt):
        sj = jnp.sum(qf * new_ref[j:j + 1, 0:w], axis=-1, keepdims=True)
        if bias_new is not None:
            sj = sj + bias_new[j]
        s_new.append(jnp.where(t_row >= j, sj, NEG))
    m = jnp.max(s, axis=-1, keepdims=True)
    for sj in s_new:
        m = jnp.maximum(m, sj)
    p = jnp.exp2(s - m)
    p_new = [jnp.where(t_row >= j, jnp.exp2(s_new[j] - m), 0.0) for j in range(nt)]
    l = jnp.sum(p, axis=-1, keepdims=True)
    for pj in p_new:
        l = l + pj
    pb = p.astype(BF16)
    acc = None
    for i, pg in enumerate(pages):
        part = pv_page(pb[:, i * PAGE_SIZE:(i + 1) * PAGE_SIZE], pg)
        acc = part if acc is None else acc + part
    for j in range(nt):
        acc = acc + p_new[j] * new_ref[j:j + 1, w:2 * w]
    return acc / jnp.maximum(l, TINY)


def _diff_page_slab(pg, first):
    return jnp.concatenate([pg[pl.ds(first + hd, PAGE_SIZE, stride=2 * A_HEADS), :] for hd in range(A_HEADS)],
                           axis=1).astype(BF16)


def _diff_score_page(q, pg):
    return lax.dot_general(q, _diff_page_slab(pg, 0), NT_DIMS, preferred_element_type=F32)


def _diff_pv_page(p, pg):
    return jnp.dot(p, _diff_page_slab(pg, A_HEADS), preferred_element_type=F32)


def _fox_score_page(q, pg):
    return jnp.dot(q, pg[0:B_HEADS * HD, :].astype(BF16), preferred_element_type=F32)


def _fox_pv_page(p, pg):
    return lax.dot_general(p, pg[B_HEADS * HD:2 * B_HEADS * HD, :].astype(BF16), NT_DIMS,
                           preferred_element_type=F32)


def _even_sample_body(pt_ref, qa_ref, qb_ref, dnew_ref, fnew_ref, lnew_ref, lam_ref, *rest, lam_init):
    del pt_ref
    dpages = rest[0:N_PAGES]
    fpages = rest[N_PAGES:2 * N_PAGES]
    lpages = rest[2 * N_PAGES:3 * N_PAGES]
    o_ref = rest[3 * N_PAGES]
    nt = dnew_ref.shape[0]
    nrow = qa_ref.shape[0]
    w = qa_ref.shape[1]
    rid = lax.broadcasted_iota(jnp.int32, (nrow, 1), 0)
    t_row = rid // 8
    h_row = rid % 8
    lane = lax.broadcasted_iota(jnp.int32, (nrow, w), 1)

    res = _paged_attend(qa_ref, dpages, _diff_score_page, _diff_pv_page, dnew_ref, None, None, t_row)
    lam = _lambda_value(lam_ref, lam_init)
    coef = jnp.where(h_row % 2 == 0, 1.0, -lam)
    wgt = jnp.where(lane // (2 * HD) == h_row // 2, coef, 0.0)
    o_ref[:, 0:w] = jnp.sum((res * wgt).reshape(nt, 8, w), axis=1)

    lt = jnp.concatenate([pg[...] for pg in lpages], axis=1)
    cum = _cumsum_lanes(lt)
    tot = jnp.concatenate([jnp.sum(lt, axis=1, keepdims=True)] * nt, axis=0)
    lnew = lnew_ref[...]
    tix = lax.broadcasted_iota(jnp.int32, lnew.shape, 0)
    cnew = jnp.zeros_like(lnew)
    for j in range(nt):
        cnew = cnew + jnp.where(tix >= j, lnew[j:j + 1, :], 0.0)
    hsel = lax.broadcasted_iota(jnp.int32, (nrow, B_HEADS), 1) == h_row
    cnew_rows = jnp.broadcast_to(cnew[:, None, :], (nt, 8, B_HEADS)).reshape(nrow, B_HEADS)
    fq_new = jnp.sum(jnp.where(hsel, cnew_rows, 0.0), axis=-1, keepdims=True)
    bias_past = ((tot + fq_new) - jnp.concatenate([cum] * nt, axis=0)) * LOG2E
    bias_new = [(fq_new - jnp.sum(jnp.where(hsel, cnew[j:j + 1, :], 0.0), axis=-1, keepdims=True)) * LOG2E
                for j in range(nt)]
    res = _paged_attend(qb_ref, fpages, _fox_score_page, _fox_pv_page, fnew_ref, bias_past, bias_new, t_row)
    wgt = jnp.where(lane // HD == h_row, 1.0, 0.0)
    o_ref[:, w:2 * w] = jnp.sum((res * wgt).reshape(nt, 8, w), axis=1)


def _even_sample_attn(e, page_table, qbd_a, qbd_b, dkv_new, fkv_new, lnew, lam_p, cache_d, cache_f, cache_l,
                      lam_init):
    nb, nt = dkv_new.shape[0], dkv_new.shape[1]

    def per_b(shape):
        nd = len(shape)
        return pl.BlockSpec((None,) + tuple(shape[1:]), lambda b, pt: (b,) + (0,) * (nd - 1))

    def page(rows, p):
        return pl.BlockSpec((None, None, rows, PAGE_SIZE), lambda b, pt: (e, pt[b, p], 0, 0))

    in_specs = [per_b(qbd_a.shape), per_b(qbd_b.shape), per_b(dkv_new.shape), per_b(fkv_new.shape),
                per_b(lnew.shape), pl.BlockSpec(lam_p.shape, lambda b, pt: (0, 0))]
    in_specs += [page(cache_d.shape[2], p) for p in range(N_PAGES)]
    in_specs += [page(cache_f.shape[2], p) for p in range(N_PAGES)]
    in_specs += [page(B_HEADS, p) for p in range(N_PAGES)]
    gs = pltpu.PrefetchScalarGridSpec(
        num_scalar_prefetch=1, grid=(nb,), in_specs=in_specs,
        out_specs=pl.BlockSpec((None, nt, 1024), lambda b, pt: (b, 0, 0)))
    return pl.pallas_call(
        functools.partial(_even_sample_body, lam_init=lam_init),
        out_shape=jax.ShapeDtypeStruct((nb, nt, 1024), F32), grid_spec=gs,
        compiler_params=_cparams(("arbitrary",)), name="even_sample")(
            page_table, qbd_a, qbd_b, dkv_new, fkv_new, lnew, lam_p,
            *([cache_d] * N_PAGES), *([cache_f] * N_PAGES), *([cache_l] * N_PAGES))


def _nsa_prompt_body(q_ref, gate_ref, kc_ref, kv_ref, oh_ref, o_ref, *, nblk):
    nq = q_ref.shape[1]
    s_len = kv_ref.shape[0]
    n_cmp = kc_ref.shape[0]
    n_sel = s_len // SEL_BLOCK
    p = pl.program_id(1)
    blocks = (p, nblk - 1 - p)
    nrow = D_REP * nq
    wq = D_HEADS * HD
    lo_q = lax.broadcasted_iota(jnp.int32, (nq, LANES), 1) < HALF
    row_q = lax.broadcasted_iota(jnp.int32, (nrow, 1), 0) % nq
    diag_mask = lax.broadcasted_iota(jnp.int32, (nrow, nq), 1) <= row_q
    pr_i = lax.broadcasted_iota(jnp.int32, (n_sel, n_cmp), 0)
    pc_i = lax.broadcasted_iota(jnp.int32, (n_sel, n_cmp), 1)
    pair_t = jnp.where(pc_i // (SEL_BLOCK // CMP_BLOCK) == pr_i, 1.0, 0.0)
    blk = lax.broadcasted_iota(jnp.int32, (n_sel, nq), 0)
    qlane = lax.broadcasted_iota(jnp.int32, (n_sel, nq), 1)
    wlen = WINDOW + nq

    def stack_q(i, base, g):
        parts = []
        for rep in range(D_REP):
            hd = g * D_REP + rep
            slab = q_ref[i, :, base + (hd // 2) * LANES:base + (hd // 2 + 1) * LANES]
            keep = lo_q if hd % 2 == 0 else jnp.logical_not(lo_q)
            parts.append(jnp.where(keep, slab, jnp.zeros_like(slab)))
        return jnp.concatenate(parts, axis=0)

    for g in range(D_KV):
        kc = kc_ref[:, g * LANES:(g + 1) * LANES].astype(BF16)
        vc = kc_ref[:, (2 + g) * LANES:(3 + g) * LANES].astype(BF16)
        q_aug, o_c, o_w = [], [], []
        for i in range(2):
            q0 = blocks[i] * nq
            qpos_col = q0 + row_q
            qc = stack_q(i, 0, g)
            qr = stack_q(i, wq, g)
            sc = lax.dot_general(qc, kc, NT_DIMS, preferred_element_type=F32)
            c_end = lax.broadcasted_iota(jnp.int32, (nrow, n_cmp), 1) * CMP_BLOCK + (CMP_BLOCK - 1)
            pc = _masked_softmax_rows(sc, c_end <= qpos_col)
            o_c.append(jnp.dot(pc.astype(BF16), vc, preferred_element_type=F32))
            imp_c = pc[0:nq] + pc[nq:2 * nq] + pc[2 * nq:3 * nq] + pc[3 * nq:4 * nq]
            imp_t = lax.dot_general(pair_t, imp_c, NT_DIMS, precision=HIGHEST, preferred_element_type=F32)
            qp = q0 + qlane
            score = jnp.where(qp // SEL_BLOCK == blk, D_REP + 1.0, jnp.where(blk * SEL_BLOCK <= qp, imp_t, -1.0))
            sel_t = _topk_rows(score, min(N_SEL, n_sel))
            pen_t = jnp.concatenate([(sel_t - 1.0) * BIG, jnp.zeros((LANES - n_sel, nq), F32)], axis=0)
            pen = pen_t.T.astype(BF16)
            q_aug.append(jnp.concatenate([qr, jnp.concatenate([pen] * D_REP, axis=0)], axis=1))
            ws = pl.multiple_of(jnp.maximum(q0 - WINDOW, 0), nq)
            wk = kv_ref[pl.ds(ws, wlen), (4 + g) * LANES:(5 + g) * LANES]
            wv = kv_ref[pl.ds(ws, wlen), (6 + g) * LANES:(7 + g) * LANES]
            wpos = ws + lax.broadcasted_iota(jnp.int32, (nrow, wlen), 1)
            wmask = (wpos <= qpos_col) & (wpos > qpos_col - WINDOW)
            sw = jnp.where(wmask, lax.dot_general(qr, wk, NT_DIMS, preferred_element_type=F32), NEG)
            pw = jnp.exp2(sw - jnp.max(sw, axis=-1, keepdims=True))
            inv = 1.0 / jnp.maximum(jnp.sum(pw, axis=-1, keepdims=True), TINY)
            o_w.append(jnp.dot(pw.astype(BF16), wv, preferred_element_type=F32) * inv)

        def score_fn(qh, kb, g=g):
            ks = pl.multiple_of(kb * nq, nq)
            k_aug = jnp.concatenate([kv_ref[pl.ds(ks, nq), g * LANES:(g + 1) * LANES], oh_ref[pl.ds(ks, nq), :]],
                                    axis=1)
            return lax.dot_general(qh, k_aug, NT_DIMS, preferred_element_type=F32)

        def v_fn(kb, g=g):
            return kv_ref[pl.ds(pl.multiple_of(kb * nq, nq), nq), (2 + g) * LANES:(3 + g) * LANES]

        o_s = _paired_causal_softmax_pv(q_aug, score_fn, v_fn, p, nblk, diag_mask)

        for i in range(2):
            gates = gate_ref[i]
            heads = []
            for rep in range(D_REP):
                hd = g * D_REP + rep
                rows = slice(rep * nq, (rep + 1) * nq)
                heads.append(gates[:, 3 * hd:3 * hd + 1] * o_c[i][rows] + gates[:, 3 * hd + 1:3 * hd + 2] * o_s[i][rows]
                             + gates[:, 3 * hd + 2:3 * hd + 3] * o_w[i][rows])
            for j in range(D_REP // 2):
                slab = g * (D_REP // 2) + j
                o_ref[i, :, slab * LANES:(slab + 1) * LANES] = jnp.where(lo_q, heads[2 * j], heads[2 * j + 1])


def _nsa_prompt(q, gates, kcd, nsa_kv, onehot):
    b, s, _ = nsa_kv.shape
    nq = TOKEN_TILE
    nblk = s // nq
    q4 = q.reshape(b, 2, s // 2, q.shape[-1])
    g4 = gates.reshape(b, 2, s // 2, gates.shape[-1])
    out = pl.pallas_call(
        functools.partial(_nsa_prompt_body, nblk=nblk),
        out_shape=jax.ShapeDtypeStruct((b, 2, s // 2, D_HEADS * HD), F32), grid=(b, nblk // 2),
        in_specs=[pl.BlockSpec((None, 2, nq, q.shape[-1]), lambda bi, p: (bi, 0, p, 0)),
                  pl.BlockSpec((None, 2, nq, gates.shape[-1]), lambda bi, p: (bi, 0, p, 0)),
                  pl.BlockSpec((None,) + kcd.shape[1:], lambda bi, p: (bi, 0, 0)),
                  pl.BlockSpec((None, s, nsa_kv.shape[-1]), lambda bi, p: (bi, 0, 0)),
                  pl.BlockSpec(onehot.shape, lambda bi, p: (0, 0))],
        out_specs=pl.BlockSpec((None, 2, nq, D_HEADS * HD), lambda bi, p: (bi, 0, p, 0)),
        compiler_params=_cparams(("arbitrary", "arbitrary")), name="nsa_prompt")(q4, g4, kcd, nsa_kv, onehot)
    return out.reshape(b * s, D_HEADS * HD)


def _nsa_sample_body(pt_ref, qc_ref, qr_ref, gate_ref, snew_ref, wnew_ref, wst_ref, avg_ref, *rest):
    del pt_ref
    o_ref = rest[-1]
    for j in range(NSA_DECODE_ROWS):
        pages = rest[2 * N_PAGES * j:2 * N_PAGES * (j + 1)]
        _nsa_sample_row(qc_ref.at[j], qr_ref.at[j], gate_ref.at[j], snew_ref.at[j], wnew_ref.at[j], wst_ref.at[j],
                        avg_ref, pages[0:N_PAGES], pages[N_PAGES:2 * N_PAGES], o_ref.at[j])


def _nsa_sample_row(qc_ref, qr_ref, gate_ref, snew_ref, wnew_ref, wst_ref, avg_ref, cpages, spages, o_ref):
    nrow = qc_ref.shape[0]
    nt = snew_ref.shape[0]
    rid = lax.broadcasted_iota(jnp.int32, (nrow, 1), 0)
    t_row = rid // D_HEADS
    qc = qc_ref[...]
    qr = qr_ref[...]
    qrf = qr.astype(F32)

    cmp_t = jnp.concatenate([pg[...].astype(BF16) for pg in cpages], axis=1)
    means_t = jnp.dot(cmp_t, avg_ref[...], preferred_element_type=F32)
    kc_t = means_t[0:LANES].astype(BF16)
    vc_t = means_t[LANES:2 * LANES].astype(BF16)
    sc = jnp.dot(qc, kc_t, preferred_element_type=F32)
    mc = jnp.max(sc, axis=-1, keepdims=True)
    pc = jnp.exp2(sc - mc)
    pc = pc / jnp.maximum(jnp.sum(pc, axis=-1, keepdims=True), TINY)
    o_c = lax.dot_general(pc.astype(BF16), vc_t, NT_DIMS, preferred_element_type=F32)

    ncol = nt * D_KV
    gr = lax.broadcasted_iota(jnp.int32, (ncol, nrow), 0)
    gc = lax.broadcasted_iota(jnp.int32, (ncol, nrow), 1)
    gsum = jnp.where(gc // D_REP == gr, 1.0, 0.0)
    imp_c = jnp.dot(gsum, pc, precision=HIGHEST, preferred_element_type=F32)
    pr = lax.broadcasted_iota(jnp.int32, (N_PAST_SEL, N_PAST_CMP), 0)
    pcx = lax.broadcasted_iota(jnp.int32, (N_PAST_SEL, N_PAST_CMP), 1)
    pair_t = jnp.where(pcx // (SEL_BLOCK // CMP_BLOCK) == pr, 1.0, 0.0)
    imp_t = lax.dot_general(pair_t, imp_c, NT_DIMS, precision=HIGHEST, preferred_element_type=F32)
    sel_t = _topk_rows(imp_t, N_SEL - 1).astype(BF16)
    er = lax.broadcasted_iota(jnp.int32, (nrow, ncol), 0)
    ec = lax.broadcasted_iota(jnp.int32, (nrow, ncol), 1)
    g_t = jnp.where(er // D_REP == ec, 1.0, 0.0).astype(BF16)
    sel_r = lax.dot_general(g_t, sel_t, NT_DIMS, preferred_element_type=F32)
    eb_r = lax.broadcasted_iota(jnp.int32, (N_PAST_SEL, PAST_LEN), 0)
    eb_c = lax.broadcasted_iota(jnp.int32, (N_PAST_SEL, PAST_LEN), 1)
    eb = jnp.where(eb_c // SEL_BLOCK == eb_r, 1.0, 0.0).astype(BF16)
    smask = jnp.dot(sel_r.astype(BF16), eb, preferred_element_type=F32) > 0.5

    def new_rows(s_past, mask_past, new_ref):
        s_past = jnp.where(mask_past, s_past, NEG)
        s_new = [jnp.where(t_row >= j, jnp.sum(qrf * new_ref[j:j + 1, 0:LANES], axis=-1, keepdims=True), NEG)
                 for j in range(nt)]
        m = jnp.max(s_past, axis=-1, keepdims=True)
        for sj in s_new:
            m = jnp.maximum(m, sj)
        p = jnp.where(mask_past, jnp.exp2(s_past - m), 0.0)
        p_new = [jnp.where(t_row >= j, jnp.exp2(s_new[j] - m), 0.0) for j in range(nt)]
        l = jnp.sum(p, axis=-1, keepdims=True)
        for pj in p_new:
            l = l + pj
        inv = 1.0 / jnp.maximum(l, TINY)
        return p * inv, [pj * inv for pj in p_new]

    s_parts = [jnp.dot(qr, pg[0:LANES, :].astype(BF16), preferred_element_type=F32) for pg in spages]
    ps, ps_new = new_rows(jnp.concatenate(s_parts, axis=1), smask, snew_ref)
    psb = ps.astype(BF16)
    o_s = None
    for i, pg in enumerate(spages):
        part = lax.dot_general(psb[:, i * PAGE_SIZE:(i + 1) * PAGE_SIZE], pg[LANES:2 * LANES, :].astype(BF16),
                               NT_DIMS, preferred_element_type=F32)
        o_s = part if o_s is None else o_s + part
    for j in range(nt):
        o_s = o_s + ps_new[j] * snew_ref[j:j + 1, LANES:2 * LANES]

    wb = wst_ref.shape[1]
    wk_t = wst_ref[0:LANES, :].astype(BF16)
    wv_t = wst_ref[LANES:2 * LANES, :].astype(BF16)
    jpos = lax.broadcasted_iota(jnp.int32, (nrow, wb), 1)
    wmask = jpos > t_row - (WINDOW - wb)
    pw, pw_new = new_rows(jnp.dot(qr, wk_t, preferred_element_type=F32), wmask, wnew_ref)
    o_w = lax.dot_general(pw.astype(BF16), wv_t, NT_DIMS, preferred_element_type=F32)
    for j in range(nt):
        o_w = o_w + pw_new[j] * wnew_ref[j:j + 1, LANES:2 * LANES]

    g = gate_ref[...]
    o_ref[...] = g[:, 0:1] * o_c + g[:, 1:2] * o_s + g[:, 2:3] * o_w


def _nsa_sample_attn(o, page_table, qc_bd, qr_bd, gates, snew, wnew, win_state, cache_c, cache_s):
    nb = qc_bd.shape[0]
    rows = NSA_DECODE_ROWS
    kv2 = 2 * D_KV * HD
    key = jnp.arange(PAST_LEN)[:, None] // CMP_BLOCK
    avg = jnp.where(key == jnp.arange(N_PAST_CMP)[None, :], 1.0 / CMP_BLOCK, 0.0).astype(BF16)

    def per_b(shape):
        nd = len(shape)
        return pl.BlockSpec((rows,) + tuple(shape[1:]), lambda b, pt: (b,) + (0,) * (nd - 1))

    def page(j, p):
        return pl.BlockSpec((None, None, kv2, PAGE_SIZE), lambda b, pt: (o, pt[b * rows + j, p], 0, 0))

    in_specs = [per_b(qc_bd.shape), per_b(qr_bd.shape), per_b(gates.shape), per_b(snew.shape), per_b(wnew.shape),
                pl.BlockSpec((None, rows) + win_state.shape[2:], lambda b, pt: (o, b, 0, 0)),
                pl.BlockSpec(avg.shape, lambda b, pt: (0, 0))]
    page_args = []
    for j in range(rows):
        in_specs += [page(j, p) for p in range(N_PAGES)] + [page(j, p) for p in range(N_PAGES)]
        page_args += [cache_c] * N_PAGES + [cache_s] * N_PAGES
    gs = pltpu.PrefetchScalarGridSpec(
        num_scalar_prefetch=1, grid=(nb // rows,), in_specs=in_specs,
        out_specs=pl.BlockSpec((rows,) + qc_bd.shape[1:], lambda b, pt: (b, 0, 0)))
    return pl.pallas_call(
        _nsa_sample_body, out_shape=jax.ShapeDtypeStruct(qc_bd.shape, F32), grid_spec=gs,
        compiler_params=_cparams(("arbitrary",)), name="nsa_sample")(
            page_table, qc_bd, qr_bd, gates, snew, wnew, win_state, avg, *page_args)


def _rope_tables(pos):
    half = ROT_DIM // 2
    inv = 1.0 / (ROPE_THETA ** (jnp.arange(half, dtype=F32) * 2.0 / ROT_DIM))
    ang = pos.astype(F32)[:, None] * inv[None, :]
    cos, sin = jnp.cos(ang), jnp.sin(ang)
    n = pos.shape[0]
    zeros8 = jnp.zeros((n, half), F32)
    rest0 = jnp.zeros((n, HD - ROT_DIM), F32)
    c = jnp.concatenate([cos, cos, jnp.ones((n, HD - ROT_DIM), F32)], axis=1)
    s1 = jnp.concatenate([-sin, zeros8, rest0], axis=1)
    s2 = jnp.concatenate([zeros8, sin, rest0], axis=1)
    return tuple(jnp.concatenate([t, t], axis=1) for t in (c, s1, s2))


def _block_diag_rows(q, group):
    nb, nt, w = q.shape
    keep = (jnp.arange(w)[None, :] // HD) == jnp.arange(8)[:, None]
    return jnp.where(keep[None, None], q[:, :, None, :], jnp.zeros((), q.dtype)).reshape(nb, nt * 8, w)


def _nsa_rows(q):
    nb, nt, _ = q.shape
    qh = q.reshape(nb, nt, D_HEADS, HD)
    z = jnp.zeros_like(qh)
    first = (jnp.arange(D_HEADS) < D_REP)[None, None, :, None]
    out = jnp.concatenate([jnp.where(first, qh, z), jnp.where(first, z, qh)], axis=-1)
    return out.reshape(nb, nt * D_HEADS, 2 * HD)


def kernel(x_prompt, x_sample, cache_diff_kv, cache_fox_kv, cache_fox_logf, cache_nsa_cmp_kv, cache_nsa_sel_kv,
           state_nsa_win_kv, page_table, norm_g, ffn_w_in, ffn_w_out, even_w_in, even_w_out, fox_b_f, diff_lambda,
           diff_subln_g, odd_w_in, odd_w_out, gmlp_ln_g, gmlp_ln_b, gmlp_w_s, gmlp_b_s):
    bsz, seq, _ = x_prompt.shape
    nb, nt, _ = x_sample.shape
    tp = bsz * seq
    ts = nb * nt
    n_pool = cache_diff_kv.shape[1]

    yp = x_prompt.reshape(tp, D_MODEL)
    ys = x_sample.reshape(ts, D_MODEL)
    tab_p = _rope_tables(jnp.arange(seq))
    tab_s = _rope_tables(PAST_LEN + (jnp.arange(ts) % nt))
    win_bf = ffn_w_in.astype(BF16)
    wout_bf = ffn_w_out.astype(BF16)
    nblk_p = seq // TOKEN_TILE
    sel_onehot = (jnp.arange(seq)[:, None] // SEL_BLOCK == jnp.arange(LANES)[None, :]).astype(BF16)

    kv2 = 2 * D_KV * HD
    keys_last = (0, 1, 3, 4, 5, 2)
    cd = cache_diff_kv.reshape(cache_diff_kv.shape[0], n_pool, PAGE_SIZE * 2 * A_HEADS, 2 * HD)
    cf = cache_fox_kv.transpose(keys_last).reshape(cache_fox_kv.shape[0], n_pool, 2 * B_HEADS * HD, PAGE_SIZE)
    cl = cache_fox_logf.transpose(0, 1, 3, 2)
    cc = cache_nsa_cmp_kv.transpose(keys_last).reshape(cache_nsa_cmp_kv.shape[0], n_pool, kv2, PAGE_SIZE)
    cs = cache_nsa_sel_kv.transpose(keys_last).reshape(cache_nsa_sel_kv.shape[0], n_pool, kv2, PAGE_SIZE)
    wstate = state_nsa_win_kv.transpose(keys_last).reshape(state_nsa_win_kv.shape[0], nb, kv2,
                                                           state_nsa_win_kv.shape[2])

    w512 = 512
    outs = {k: [] for k in ("dkv_s", "fkv_s", "fl_p", "fl_s", "ckv_s", "skv_s", "wkv_s", "gv_s")}
    even_stack = odd_stack = None
    for i in range(DEPTH):
        nrm = norm_g[i]
        if i % 2 == 0:
            e = i // 2
            lam_init = 0.8 - 0.6 * math.exp(-0.3 * i)
            w = even_w_in[e]
            aq, ak, av, bq, bk, bv, bfc = jnp.split(w, [512, 1024, 1536, 2048, 2560, 3072], axis=1)
            wp = jnp.concatenate([aq, bq, ak, av, bk, bv, bfc,
                                  jnp.zeros((D_MODEL, EVEN_COLS_PAD - w.shape[1]), F32)], axis=1).astype(BF16)
            bf_pad = jnp.concatenate([fox_b_f[e], jnp.zeros((LANES - B_HEADS,), F32)])[None, :]
            lam_p = diff_lambda[e]
            sg = diff_subln_g[e][None, :]
            wmo = even_w_out[e].astype(BF16)

            hp, q, dkv, fkv, kb, vb, logf = _even_pre(yp, nrm, win_bf, wout_bf, i, wp, bf_pad, tab_p, nblk=nblk_p,
                                                      prev=even_stack)
            even_stack = (dkv, fkv)
            k3, v3 = (a.reshape(bsz, seq, 1024) for a in (kb, vb))
            oa = _pair_attn_prompt(q, k3, v3, lam_p, fox=False, lam_init=lam_init)
            logf_t = logf.reshape(bsz, seq, B_HEADS).transpose(0, 2, 1).reshape(bsz * B_HEADS, seq)
            frow = _cumsum_rows_of_lanes(logf_t).reshape(bsz, B_HEADS // 2, 2, seq)
            ob = _pair_attn_prompt(q, k3, v3, frow, fox=True)
            yp = _even_post(oa, ob, sg, wmo, hp, nrm, win_bf, wout_bf, i, 1.0 - lam_init, nblk=nblk_p)
            outs["fl_p"].append(logf.reshape(bsz, seq, B_HEADS))

            hs, q, dkv, fkv, kb, vb, logf = _even_pre(ys, nrm, win_bf, wout_bf, i, wp, bf_pad, tab_s)
            qs3 = q.reshape(nb, nt, 1024)
            o = _even_sample_attn(e, page_table, _block_diag_rows(qs3[:, :, :w512], 8),
                                  _block_diag_rows(qs3[:, :, w512:], 8), dkv.reshape(nb, nt, 1024),
                                  fkv.reshape(nb, nt, 1024), logf.reshape(nb, nt, B_HEADS), lam_p, cd, cf,
                                  cl, lam_init).reshape(ts, 1024)
            ys = _even_post(o[:, :w512], o[:, w512:], sg, wmo, hs, nrm, win_bf, wout_bf, i, 1.0 - lam_init)
            outs["dkv_s"].append(dkv.reshape(nb, nt, 2, A_HEADS, 2 * HD))
            outs["fkv_s"].append(fkv.reshape(nb, nt, 2, B_HEADS, HD))
            outs["fl_s"].append(logf.reshape(nb, nt, B_HEADS))
        else:
            o_idx = i // 2
            w = odd_w_in[o_idx]
            wp = jnp.concatenate([w, jnp.zeros((D_MODEL, ODD_COLS_PAD - w.shape[1]), F32)], axis=1).astype(BF16)
            lng = gmlp_ln_g[o_idx][None, :]
            lnb = gmlp_ln_b[o_idx][None, :]
            wmo = odd_w_out[o_idx].astype(BF16)
            ws = gmlp_w_s[o_idx]
            bs = gmlp_b_s[o_idx]
            tm = TOKEN_TILE
            kv_shape = (2, D_KV, HD)

            hp, u, v, q, kcd, ckv, skv, wkv, nsa_kv, gates = _odd_pre(yp, nrm, win_bf, wout_bf, i, wp, lng, lnb, tab_p,
                                                                      nblk=nblk_p, prev=odd_stack)
            odd_stack = (ckv, skv, wkv)
            od = _nsa_prompt(q, gates, kcd.reshape(bsz, seq // CMP_BLOCK, 4 * LANES),
                             nsa_kv.reshape(bsz, seq, 1024), sel_onehot)
            bs_p = jnp.tile(jnp.repeat(bs.T, C_GW, axis=1), (tm // CHUNK, 1))
            yp = _odd_post(u, v, od, ws, bs_p, wmo, hp, nrm, win_bf, wout_bf, i, True, nblk=nblk_p)

            hs, u, v, q, ckv, skv, wkv, gates = _odd_pre(ys, nrm, win_bf, wout_bf, i, wp, lng, lnb, tab_s)
            q3 = q.reshape(nb, nt, 1024)
            kv2 = 2 * D_KV * HD
            od_raw = _nsa_sample_attn(o_idx, page_table, _nsa_rows(q3[:, :, :512]), _nsa_rows(q3[:, :, 512:]),
                                      gates.reshape(nb, nt * D_HEADS, 3), skv.reshape(nb, nt, kv2),
                                      wkv.reshape(nb, nt, kv2), wstate, cc, cs)
            od4 = od_raw.reshape(nb, nt, D_HEADS, 2, HD)
            first = (jnp.arange(D_HEADS) < D_REP)[None, None, :, None]
            od = jnp.where(first, od4[:, :, :, 0], od4[:, :, :, 1]).reshape(ts, 512)
            tpos = jnp.arange(nt)
            coefs = []
            for k in range(nt):
                val = jnp.where((tpos >= k)[None, :], ws[:, tpos, jnp.maximum(tpos - k, 0)], 0.0)
                coefs.append(jnp.tile(jnp.repeat(val.T, C_GW, axis=1), (tm // nt, 1)))
            bs_s = jnp.tile(jnp.repeat(bs[:, :nt].T, C_GW, axis=1), (tm // nt, 1))
            ys = _odd_post(u, v, od, jnp.stack(coefs), bs_s, wmo, hs, nrm, win_bf, wout_bf, i, False)
            outs["ckv_s"].append(ckv.reshape((nb, nt) + kv_shape))
            outs["skv_s"].append(skv.reshape((nb, nt) + kv_shape))
            outs["wkv_s"].append(wkv.reshape((nb, nt) + kv_shape))
            outs["gv_s"].append(v.reshape(nb, nt, C_CH))

    st = lambda k: jnp.stack(outs[k], axis=0)
    n_even, n_odd = (DEPTH + 1) // 2, DEPTH // 2
    tokens_first = (0, 1, 5, 2, 3, 4)
    dkv_p = even_stack[0].reshape(n_even, bsz, seq, 2, A_HEADS, 2 * HD)
    fkv_p = even_stack[1].reshape(n_even, bsz, 2, B_HEADS, HD, seq).transpose(tokens_first)
    win_rows = min(WINDOW, seq)
    ckv_p, skv_p, wkv_p = (a.reshape(n_odd, bsz, 2, D_KV, HD, seq).transpose(tokens_first) for a in odd_stack)
    return (yp.reshape(bsz, seq, D_MODEL), ys.reshape(nb, nt, D_MODEL), dkv_p, st("dkv_s"), fkv_p,
            st("fkv_s"), st("fl_p"), st("fl_s"), ckv_p, st("ckv_s"), skv_p, st("skv_s"), wkv_p[:, :, seq - win_rows:],
            st("wkv_s"), st("gv_s"))
```

```python
import functools
import math

import jax
import jax.numpy as jnp
from jax import lax
from jax.experimental import pallas as pl
from jax.experimental.pallas import tpu as pltpu

F32 = jnp.float32
BF16 = jnp.bfloat16

D_MODEL = 1024
DEPTH = 4
HD = 64
ROT_DIM = HD // 4
ROPE_THETA = 500000.0
A_HEADS = 4
B_HEADS = 8
C_CH = 512
C_GROUPS = 4
C_GW = C_CH // C_GROUPS
CHUNK = 128
D_HEADS = 8
D_KV = 2
D_REP = D_HEADS // D_KV
CMP_BLOCK = 32
SEL_BLOCK = 64
N_SEL = 8
WINDOW = 512
D_FF = 2816
EPS = 1e-6
NEG = -1e30
TINY = 1e-30
PAGE_SIZE = 128
PAST_LEN = 2048
N_PAGES = PAST_LEN // PAGE_SIZE
N_PAST_SEL = PAST_LEN // SEL_BLOCK
N_PAST_CMP = PAST_LEN // CMP_BLOCK
LOG2E = 1.4426950408889634
QSCALE = HD ** -0.5 * LOG2E
BIG = 1e30

LANES = 128
HALF = LANES // 2
VMEM_LIMIT = 56 * 2 ** 20
TOKEN_TILE = 256
EVEN_COLS_PAD = 3200
ODD_COLS_PAD = 2432
NSA_DECODE_ROWS = 2
NT_DIMS = (((1,), (1,)), ((), ()))
HIGHEST = lax.Precision.HIGHEST


def _cparams(sem=None):
    return pltpu.CompilerParams(dimension_semantics=sem, vmem_limit_bytes=VMEM_LIMIT)


def _resident(shape):
    nd = len(shape)
    return pl.BlockSpec(shape, lambda *_: (0,) * nd, pipeline_mode=pl.Buffered(1))


def _rms(x, g):
    return x * lax.rsqrt(jnp.mean(x * x, axis=-1, keepdims=True) + EPS) * g


def _lane_tile(t, width):
    return jnp.concatenate([t] * (width // LANES), axis=1)


def _rope(x, c, s1, s2):
    w = x.shape[1]
    return (x * _lane_tile(c, w) + pltpu.roll(x, w - ROT_DIM // 2, 1) * _lane_tile(s1, w)
            + pltpu.roll(x, ROT_DIM // 2, 1) * _lane_tile(s2, w))


def _dup_halves(x):
    r = pltpu.roll(x, HALF, 1)
    lo = lax.broadcasted_iota(jnp.int32, x.shape, 1) < HALF
    return jnp.where(lo, x, r), jnp.where(lo, r, x)


def _ffn_half(x, g_pre, g_post, win_ref, wout_ref):
    xn = _rms(x, g_pre).astype(BF16)
    gu = jnp.dot(xn, win_ref[...], preferred_element_type=F32)
    act = (jax.nn.silu(gu[:, 0:D_FF]) * gu[:, D_FF:2 * D_FF]).astype(BF16)
    acc = jnp.dot(act, wout_ref[...], preferred_element_type=F32)
    return x + 0.5 * _rms(acc, g_post)


def _cumsum_lanes(x):
    n = x.shape[1]
    lane = lax.broadcasted_iota(jnp.int32, x.shape, 1)
    s = 1
    while s < n:
        x = x + jnp.where(lane >= s, pltpu.roll(x, s, 1), 0.0)
        s *= 2
    return x


def _topk_rows(score, k):
    n = score.shape[0]
    row = lax.broadcasted_iota(jnp.int32, score.shape, 0).astype(F32)
    sel = jnp.zeros(score.shape, F32)
    for _ in range(k):
        m = jnp.max(score, axis=0, keepdims=True)
        first = jnp.min(jnp.where(score == m, row, float(n)), axis=0, keepdims=True)
        pick = row == first
        sel = jnp.where(pick, 1.0, sel)
        score = jnp.where(pick, -2.0, score)
    return sel


def _masked_softmax_rows(s, mask):
    s = jnp.where(mask, s, NEG)
    m = jnp.max(s, axis=-1, keepdims=True)
    p = jnp.where(mask, jnp.exp2(s - m), 0.0)
    return p / jnp.maximum(jnp.sum(p, axis=-1, keepdims=True), TINY)


def _even_pre_body(*refs, final_layout):
    x_ref, nrm_ref, win_ref, wout_ref, wp_ref, bf_ref, c_ref, s1_ref, s2_ref = refs[:9]
    h_ref, q_ref, dkv_ref, fkv_ref, k_ref, v_ref, logf_ref = refs[-7:]
    tm = x_ref.shape[0]
    h = _ffn_half(x_ref[...], nrm_ref[0:1, :], nrm_ref[1:2, :], win_ref, wout_ref)
    h_ref[...] = h
    hn = _rms(h, nrm_ref[2:3, :]).astype(BF16)
    z = jnp.dot(hn, wp_ref[...], preferred_element_type=F32)
    c, s1, s2 = c_ref[...], s1_ref[...], s2_ref[...]
    w = A_HEADS * 2 * HD
    aq = _rope(z[:, 0:w], c, s1, s2)
    bq = z[:, w:2 * w]
    ak = _rope(z[:, 2 * w:3 * w], c, s1, s2)
    av = z[:, 3 * w:4 * w]
    bk = z[:, 4 * w:5 * w]
    bv = z[:, 5 * w:6 * w]
    q_ref[:, 0:w] = (aq * QSCALE).astype(BF16)
    q_ref[:, w:2 * w] = (bq * QSCALE).astype(BF16)
    if final_layout:
        for hd in range(A_HEADS):
            dkv_ref[pl.ds(hd, tm, stride=2 * A_HEADS), :] = ak[:, hd * 2 * HD:(hd + 1) * 2 * HD]
            dkv_ref[pl.ds(A_HEADS + hd, tm, stride=2 * A_HEADS), :] = av[:, hd * 2 * HD:(hd + 1) * 2 * HD]
        fkv_ref[0:w, :] = bk.T
        fkv_ref[w:2 * w, :] = bv.T
    else:
        dkv_ref[:, 0:w] = ak
        dkv_ref[:, w:2 * w] = av
        fkv_ref[:, 0:w] = bk
        fkv_ref[:, w:2 * w] = bv
    k_ref[:, 0:w] = ak.astype(BF16)
    k_ref[:, w:2 * w] = bk.astype(BF16)
    v_ref[:, 0:w] = av.astype(BF16)
    v_ref[:, w:2 * w] = bv.astype(BF16)
    lf = jax.nn.log_sigmoid(z[:, 6 * w:6 * w + LANES] + bf_ref[...])
    logf_ref[...] = lf[:, 0:B_HEADS]


def _odd_pre_body(*refs, final_layout):
    x_ref, nrm_ref, win_ref, wout_ref, wp_ref, lng_ref, lnb_ref, c_ref, s1_ref, s2_ref = refs[:10]
    if final_layout:
        h_ref, u_ref, v_ref, q_ref, kcd_ref, cmp_ref, sel_ref, win_kv_ref, nsa_ref, gate_ref = refs[-10:]
    else:
        h_ref, u_ref, v_ref, q_ref, cmp_ref, sel_ref, win_kv_ref, gate_ref = refs[-8:]
    h = _ffn_half(x_ref[...], nrm_ref[0:1, :], nrm_ref[1:2, :], win_ref, wout_ref)
    h_ref[...] = h
    hn = _rms(h, nrm_ref[2:3, :]).astype(BF16)
    z = jnp.dot(hn, wp_ref[...], preferred_element_type=F32)
    c, s1, s2 = c_ref[...], s1_ref[...], s2_ref[...]
    u_ref[...] = jax.nn.gelu(z[:, 0:C_CH])
    gv = jax.nn.gelu(z[:, C_CH:2 * C_CH])
    for g in range(C_GROUPS):
        blk = gv[:, g * C_GW:(g + 1) * C_GW]
        mu = jnp.mean(blk, axis=-1, keepdims=True)
        var = jnp.mean(jnp.square(blk - mu), axis=-1, keepdims=True)
        v_ref[:, g * C_GW:(g + 1) * C_GW] = ((blk - mu) * lax.rsqrt(var + EPS) * lng_ref[:, g * C_GW:(g + 1) * C_GW]
                                             + lnb_ref[:, g * C_GW:(g + 1) * C_GW])
    o = 2 * C_CH
    wq = D_HEADS * HD
    q = z[:, o:o + wq]
    q_ref[:, 0:wq] = (q * QSCALE).astype(BF16)
    q_ref[:, wq:2 * wq] = (_rope(q, c, s1, s2) * QSCALE).astype(BF16)
    o += wq
    kvw = D_KV * HD
    ck, cv = z[:, o:o + kvw], z[:, o + kvw:o + 2 * kvw]
    sk, sv = _rope(z[:, o + 2 * kvw:o + 3 * kvw], c, s1, s2), z[:, o + 3 * kvw:o + 4 * kvw]
    wk, wv = _rope(z[:, o + 4 * kvw:o + 5 * kvw], c, s1, s2), z[:, o + 5 * kvw:o + 6 * kvw]
    if final_layout:
        tm = x_ref.shape[0]
        ckv = jnp.concatenate([ck, cv], axis=1)
        cmp_ref[...] = ckv.T
        sel_ref[...] = jnp.concatenate([sk, sv], axis=1).T
        win_kv_ref[...] = jnp.concatenate([wk, wv], axis=1).T
        means = jnp.sum(ckv.reshape(tm // CMP_BLOCK, CMP_BLOCK, 2 * kvw), axis=1) * (1.0 / CMP_BLOCK)
        for j in range(2):
            d0, d1 = _dup_halves(means[:, j * LANES:(j + 1) * LANES])
            kcd_ref[:, (2 * j) * LANES:(2 * j + 1) * LANES] = d0
            kcd_ref[:, (2 * j + 1) * LANES:(2 * j + 2) * LANES] = d1
        for j, arr in enumerate((sk, sv, wk, wv)):
            d0, d1 = _dup_halves(arr)
            nsa_ref[:, (2 * j) * LANES:(2 * j + 1) * LANES] = d0.astype(BF16)
            nsa_ref[:, (2 * j + 1) * LANES:(2 * j + 2) * LANES] = d1.astype(BF16)
    else:
        cmp_ref[:, 0:kvw] = ck
        cmp_ref[:, kvw:2 * kvw] = cv
        sel_ref[:, 0:kvw] = sk
        sel_ref[:, kvw:2 * kvw] = sv
        win_kv_ref[:, 0:kvw] = wk
        win_kv_ref[:, kvw:2 * kvw] = wv
    o += 6 * kvw
    gate_ref[...] = jax.nn.sigmoid(z[:, o:o + LANES])[:, 0:3 * D_HEADS]


def _mixer_post(o, wmo_ref, h, nrm_ref, win_ref, wout_ref):
    m = jnp.dot(o.astype(BF16), wmo_ref[...], preferred_element_type=F32)
    h2 = h + _rms(m, nrm_ref[3:4, :])
    return _ffn_half(h2, nrm_ref[4:5, :], nrm_ref[5:6, :], win_ref, wout_ref)


def _even_post_body(oa_ref, ob_ref, sg_ref, wmo_ref, h_ref, nrm_ref, win_ref, wout_ref, y_ref, *, out_scale):
    sg = sg_ref[...]
    parts = []
    for hd in range(A_HEADS):
        blk = oa_ref[:, hd * 2 * HD:(hd + 1) * 2 * HD]
        parts.append(_rms(blk, sg) * out_scale)
    parts.append(ob_ref[...])
    o = jnp.concatenate(parts, axis=1)
    y_ref[...] = _mixer_post(o, wmo_ref, h_ref[...], nrm_ref, win_ref, wout_ref)


def _odd_post_body(u_ref, v_ref, od_ref, ws_ref, bs_ref, wmo_ref, h_ref, nrm_ref, win_ref, wout_ref, y_ref,
                   *, chunked):
    tm = u_ref.shape[0]
    parts = []
    if chunked:
        keep = (lax.broadcasted_iota(jnp.int32, (CHUNK, CHUNK), 1) <= lax.broadcasted_iota(jnp.int32, (CHUNK, CHUNK), 0))
        for g in range(C_GROUPS):
            wm = jnp.where(keep, ws_ref[g], 0.0).astype(BF16)
            vg = v_ref[:, g * C_GW:(g + 1) * C_GW].astype(BF16)
            mix = jnp.concatenate([jnp.dot(wm, vg[ch * CHUNK:(ch + 1) * CHUNK], preferred_element_type=F32)
                                   for ch in range(tm // CHUNK)], axis=0) + bs_ref[:, g * C_GW:(g + 1) * C_GW]
            parts.append(u_ref[:, g * C_GW:(g + 1) * C_GW] * mix)
    else:
        v = v_ref[...]
        mix = ws_ref[0] * v + bs_ref[...]
        for k in range(1, ws_ref.shape[0]):
            mix = mix + ws_ref[k] * pltpu.roll(v, k, 0)
        parts.append(u_ref[...] * mix)
    parts.append(od_ref[...])
    o = jnp.concatenate(parts, axis=1)
    y_ref[...] = _mixer_post(o, wmo_ref, h_ref[...], nrm_ref, win_ref, wout_ref)


def _row_spec(tm, width):
    return pl.BlockSpec((tm, width), lambda i: (i, 0))


def _pair_slot(i, nblk):
    half = nblk // 2
    qi = i % nblk
    return (i // nblk) * nblk + jnp.where(qi < half, qi, half + nblk - 1 - qi)


def _slot_spec(tm, width, nblk):
    if nblk is None:
        return _row_spec(tm, width)
    return pl.BlockSpec((tm, width), lambda i: (_pair_slot(i, nblk), 0))


def _table_spec(tm, rows):
    nrep = rows // tm
    return pl.BlockSpec((tm, LANES), lambda i: (i % nrep, 0))


def _ffn_specs(win_all, wout_all, layer, half):
    def pick(shape):
        return pl.BlockSpec((None, None) + tuple(shape[2:]), lambda *_: (layer, half, 0, 0),
                            pipeline_mode=pl.Buffered(1))
    return [pick(win_all.shape), pick(wout_all.shape)]


def _stacked(layer, nlayers, prev, shape, block, index):
    spec = pl.BlockSpec((None,) + block, lambda i: (layer,) + index(i))
    return jax.ShapeDtypeStruct((nlayers,) + shape, F32), spec, prev


def _even_pre(x, nrm, win_all, wout_all, layer, wp, bf_pad, tables, nblk=None, prev=None):
    t = x.shape[0]
    tm = TOKEN_TILE
    tabs = [_table_spec(tm, tables[0].shape[0])] * 3
    final = nblk is not None
    sds = jax.ShapeDtypeStruct
    aliases, extra_in, extra_specs = {}, [], []
    if final:
        n_even, e, bsz = (DEPTH + 1) // 2, layer // 2, t // (nblk * tm)
        dkv = _stacked(e, n_even, None if prev is None else prev[0], (t * 2 * A_HEADS, 2 * HD),
                       (tm * 2 * A_HEADS, 2 * HD), lambda i: (i, 0))
        fkv = _stacked(e, n_even, None if prev is None else prev[1], (bsz, 2 * B_HEADS * HD, nblk * tm),
                       (None, 2 * B_HEADS * HD, tm), lambda i: (i // nblk, 0, i % nblk))
        for out_idx, st in ((2, dkv), (3, fkv)):
            if st[2] is not None:
                aliases[9 + len(extra_in)] = out_idx
                extra_in.append(st[2])
                extra_specs.append(pl.BlockSpec(memory_space=pl.ANY))
        kv_shapes, kv_specs = (dkv[0], fkv[0]), (dkv[1], fkv[1])
    else:
        kv_shapes = (sds((t, 1024), F32), sds((t, 1024), F32))
        kv_specs = (_row_spec(tm, 1024), _row_spec(tm, 1024))
    outs = (sds((t, D_MODEL), F32), sds((t, 1024), BF16)) + kv_shapes + (sds((t, 1024), BF16), sds((t, 1024), BF16),
                                                                        sds((t, B_HEADS), F32))
    return pl.pallas_call(
        functools.partial(_even_pre_body, final_layout=final), out_shape=outs, grid=(t // tm,),
        in_specs=[_row_spec(tm, D_MODEL), _resident(nrm.shape)] + _ffn_specs(win_all, wout_all, layer, 0)
        + [_resident(wp.shape), _resident(bf_pad.shape)] + tabs + extra_specs,
        out_specs=(_row_spec(tm, D_MODEL), _slot_spec(tm, 1024, nblk)) + kv_specs
        + (_row_spec(tm, 1024), _row_spec(tm, 1024), _row_spec(tm, B_HEADS)),
        input_output_aliases=aliases,
        compiler_params=_cparams(("arbitrary",)), name="even_pre")(x, nrm, win_all, wout_all, wp, bf_pad, *tables,
                                                                   *extra_in)


def _odd_pre(x, nrm, win_all, wout_all, layer, wp, lng, lnb, tables, nblk=None, prev=None):
    t = x.shape[0]
    tm = TOKEN_TILE
    tabs = [_table_spec(tm, tables[0].shape[0])] * 3
    kv2 = 2 * D_KV * HD
    final = nblk is not None
    sds = jax.ShapeDtypeStruct
    aliases, extra_in, extra_specs = {}, [], []
    head = (sds((t, D_MODEL), F32), sds((t, C_CH), F32), sds((t, C_CH), F32), sds((t, 1024), BF16))
    head_specs = (_row_spec(tm, D_MODEL), _row_spec(tm, C_CH), _row_spec(tm, C_CH), _slot_spec(tm, 1024, nblk))
    gate = (sds((t, 3 * D_HEADS), F32),)
    gate_spec = (_slot_spec(tm, 3 * D_HEADS, nblk),)
    if final:
        n_odd, o, bsz = DEPTH // 2, layer // 2, t // (nblk * tm)
        stacks = [_stacked(o, n_odd, None if prev is None else prev[j], (bsz, kv2, nblk * tm), (None, kv2, tm),
                           lambda i: (i // nblk, 0, i % nblk)) for j in range(3)]
        for j, st in enumerate(stacks):
            if st[2] is not None:
                aliases[10 + len(extra_in)] = 5 + j
                extra_in.append(st[2])
                extra_specs.append(pl.BlockSpec(memory_space=pl.ANY))
        outs = (head + (sds((t // CMP_BLOCK, 4 * LANES), F32),) + tuple(st[0] for st in stacks)
                + (sds((t, 1024), BF16),) + gate)
        out_specs = (head_specs + (_row_spec(tm // CMP_BLOCK, 4 * LANES),) + tuple(st[1] for st in stacks)
                     + (_row_spec(tm, 1024),) + gate_spec)
    else:
        outs = head + (sds((t, kv2), F32),) * 3 + gate
        out_specs = head_specs + (_row_spec(tm, kv2),) * 3 + gate_spec
    return pl.pallas_call(
        functools.partial(_odd_pre_body, final_layout=final), out_shape=outs, grid=(t // tm,),
        in_specs=[_row_spec(tm, D_MODEL), _resident(nrm.shape)] + _ffn_specs(win_all, wout_all, layer, 0)
        + [_resident(wp.shape), _resident(lng.shape), _resident(lnb.shape)] + tabs + extra_specs,
        out_specs=out_specs, input_output_aliases=aliases,
        compiler_params=_cparams(("arbitrary",)), name="odd_pre")(x, nrm, win_all, wout_all, wp, lng, lnb, *tables,
                                                                  *extra_in)


def _even_post(oa, ob, sg, wmo, h, nrm, win_all, wout_all, layer, out_scale, nblk=None):
    t = h.shape[0]
    tm = TOKEN_TILE
    return pl.pallas_call(
        functools.partial(_even_post_body, out_scale=out_scale),
        out_shape=jax.ShapeDtypeStruct((t, D_MODEL), F32), grid=(t // tm,),
        in_specs=[_slot_spec(tm, 512, nblk), _slot_spec(tm, 512, nblk), _resident(sg.shape), _resident(wmo.shape),
                  _row_spec(tm, D_MODEL), _resident(nrm.shape)] + _ffn_specs(win_all, wout_all, layer, 1),
        out_specs=_row_spec(tm, D_MODEL),
        compiler_params=_cparams(("arbitrary",)), name="even_post")(oa, ob, sg, wmo, h, nrm, win_all, wout_all)


def _odd_post(u, v, od, ws, bs_full, wmo, h, nrm, win_all, wout_all, layer, chunked, nblk=None):
    t = h.shape[0]
    tm = TOKEN_TILE
    return pl.pallas_call(
        functools.partial(_odd_post_body, chunked=chunked),
        out_shape=jax.ShapeDtypeStruct((t, D_MODEL), F32), grid=(t // tm,),
        in_specs=[_row_spec(tm, C_CH), _row_spec(tm, C_CH), _slot_spec(tm, 512, nblk), _resident(ws.shape),
                  _resident(bs_full.shape), _resident(wmo.shape), _row_spec(tm, D_MODEL), _resident(nrm.shape)]
        + _ffn_specs(win_all, wout_all, layer, 1),
        out_specs=_row_spec(tm, D_MODEL),
        compiler_params=_cparams(("arbitrary",)), name="odd_post")(u, v, od, ws, bs_full, wmo, h, nrm, win_all,
                                                                   wout_all)


def _lambda_value(lam_ref, lam_init):
    lp = lam_ref[...]
    a = jnp.sum(lp[0:1, :] * lp[1:2, :], axis=-1, keepdims=True)
    b = jnp.sum(lp[2:3, :] * lp[3:4, :], axis=-1, keepdims=True)
    return jnp.exp(a) - jnp.exp(b) + lam_init


def _cumsum_body(x_ref, o_ref):
    o_ref[...] = _cumsum_lanes(x_ref[...]) * LOG2E


def _fold(x, op):
    out = x[:, 0:LANES]
    for t in range(1, x.shape[1] // LANES):
        out = op(out, x[:, t * LANES:(t + 1) * LANES])
    return out


def _paired_causal_softmax_pv(qpair, score_fn, v_fn, p, nblk, diag_mask):
    blocks = (p, nblk - 1 - p)
    steps = []
    for u in range(nblk - 1):
        is_a = u < p
        steps.append((is_a, jnp.where(is_a, u, u - p)))
    sd = [jnp.where(diag_mask, score_fn(qpair[i], blocks[i]), NEG) for i in range(2)]
    mrun = [_fold(sd[i], jnp.maximum) for i in range(2)]
    for is_a, kb in steps:
        f = _fold(score_fn(jnp.where(is_a, qpair[0], qpair[1]), kb), jnp.maximum)
        mrun[0] = jnp.where(is_a, jnp.maximum(mrun[0], f), mrun[0])
        mrun[1] = jnp.where(is_a, mrun[1], jnp.maximum(mrun[1], f))
    m = [jnp.max(mrun[i], axis=-1, keepdims=True) for i in range(2)]
    lrun, acc = [], []
    for i in range(2):
        pr = jnp.exp2(sd[i] - m[i])
        acc.append(jnp.dot(pr.astype(BF16), v_fn(blocks[i]), preferred_element_type=F32))
        lrun.append(_fold(pr, jnp.add))
    for is_a, kb in steps:
        s = score_fn(jnp.where(is_a, qpair[0], qpair[1]), kb)
        pr = jnp.exp2(s - jnp.where(is_a, m[0], m[1]))
        pv = jnp.dot(pr.astype(BF16), v_fn(kb), preferred_element_type=F32)
        f = _fold(pr, jnp.add)
        lrun[0] = jnp.where(is_a, lrun[0] + f, lrun[0])
        lrun[1] = jnp.where(is_a, lrun[1], lrun[1] + f)
        acc[0] = jnp.where(is_a, acc[0] + pv, acc[0])
        acc[1] = jnp.where(is_a, acc[1], acc[1] + pv)
    return [acc[i] / jnp.maximum(jnp.sum(lrun[i], axis=-1, keepdims=True), TINY) for i in range(2)]


def _cumsum_rows_of_lanes(x):
    return pl.pallas_call(_cumsum_body, out_shape=jax.ShapeDtypeStruct(x.shape, F32), name="fox_cumsum",
                          compiler_params=_cparams())(x)


def _pair_attn_body(*refs, fox, lam_init, nblk):
    q_ref, k_ref, v_ref, extra_ref, o_ref = refs
    tq = q_ref.shape[1]
    p = pl.program_id(2)
    lo = lax.broadcasted_iota(jnp.int32, (tq, LANES), 1) < HALF
    diag_mask = (lax.broadcasted_iota(jnp.int32, (tq, tq), 1) <= lax.broadcasted_iota(jnp.int32, (tq, tq), 0))

    def v_fn(kb):
        return v_ref[pl.ds(pl.multiple_of(kb * tq, tq), tq), :]

    outs = []
    for c in range(2):
        def score_fn(qh, kb, c=c):
            ks = pl.multiple_of(kb * tq, tq)
            s = lax.dot_general(qh, k_ref[pl.ds(ks, tq), :], NT_DIMS, preferred_element_type=F32)
            if fox:
                s = s - extra_ref[c:c + 1, pl.ds(ks, tq)]
            return s

        qpair = []
        for i in range(2):
            x = q_ref[i]
            zero = jnp.zeros_like(x)
            qpair.append(jnp.where(lo, x, zero) if c == 0 else jnp.where(lo, zero, x))
        outs.append(_paired_causal_softmax_pv(qpair, score_fn, v_fn, p, nblk, diag_mask))
    for i in range(2):
        if fox:
            o_ref[i] = jnp.where(lo, outs[0][i], outs[1][i])
        else:
            o_ref[i] = outs[0][i] - _lambda_value(extra_ref, lam_init) * outs[1][i]


def _pair_attn_prompt(q, k, v, extra, *, fox, lam_init=0.0):
    b, s, _ = k.shape
    tq = TOKEN_TILE
    nblk = s // tq
    off = 4 if fox else 0
    q4 = q.reshape(b, 2, s // 2, q.shape[-1])
    in_specs = [pl.BlockSpec((None, 2, tq, LANES), lambda bi, j, p: (bi, 0, p, j + off)),
                pl.BlockSpec((None, s, LANES), lambda bi, j, p: (bi, 0, j + off)),
                pl.BlockSpec((None, s, LANES), lambda bi, j, p: (bi, 0, j + off))]
    if fox:
        in_specs += [pl.BlockSpec((None, None, 2, s), lambda bi, j, p: (bi, j, 0, 0))]
    else:
        in_specs += [pl.BlockSpec(extra.shape, lambda bi, j, p: (0, 0))]
    out = pl.pallas_call(
        functools.partial(_pair_attn_body, fox=fox, lam_init=lam_init, nblk=nblk),
        out_shape=jax.ShapeDtypeStruct((b, 2, s // 2, 512), F32), grid=(b, 4, nblk // 2),
        in_specs=in_specs,
        out_specs=pl.BlockSpec((None, 2, tq, LANES), lambda bi, j, p: (bi, 0, p, j)),
        compiler_params=_cparams(("arbitrary", "arbitrary", "arbitrary")),
        name="fox_prompt" if fox else "diff_prompt")(q4, k, v, extra)
    return out.reshape(b * s, 512)


def _paged_attend(q_ref, pages, score_page, pv_page, new_ref, bias_past, bias_new, t_row):
    w = q_ref.shape[1]
    qb = q_ref[...]
    s = jnp.concatenate([score_page(qb, pg) for pg in pages], axis=1)
    if bias_past is not None:
        s = s + bias_past
    qf = qb.astype(F32)
    nt = new_ref.shape[0]
    s_new = []
    for j in range(nt):
        sj = jnp.sum(qf * new_ref[j:j + 1, 0:w], axis=-1, keepdims=True)
        if bias_new is not None:
            sj = sj + bias_new[j]
        s_new.append(jnp.where(t_row >= j, sj, NEG))
    m = jnp.max(s, axis=-1, keepdims=True)
    for sj in s_new:
        m = jnp.maximum(m, sj)
    p = jnp.exp2(s - m)
    p_new = [jnp.where(t_row >= j, jnp.exp2(s_new[j] - m), 0.0) for j in range(nt)]
    l = jnp.sum(p, axis=-1, keepdims=True)
    for pj in p_new:
        l = l + pj
    pb = p.astype(BF16)
    acc = None
    for i, pg in enumerate(pages):
        part = pv_page(pb[:, i * PAGE_SIZE:(i + 1) * PAGE_SIZE], pg)
        acc = part if acc is None else acc + part
    for j in range(nt):
        acc = acc + p_new[j] * new_ref[j:j + 1, w:2 * w]
    return acc / jnp.maximum(l, TINY)


def _diff_page_slab(pg, first):
    return jnp.concatenate([pg[pl.ds(first + hd, PAGE_SIZE, stride=2 * A_HEADS), :] for hd in range(A_HEADS)],
                           axis=1).astype(BF16)


def _diff_score_page(q, pg):
    return lax.dot_general(q, _diff_page_slab(pg, 0), NT_DIMS, preferred_element_type=F32)


def _diff_pv_page(p, pg):
    return jnp.dot(p, _diff_page_slab(pg, A_HEADS), preferred_element_type=F32)


def _fox_score_page(q, pg):
    return jnp.dot(q, pg[0:B_HEADS * HD, :].astype(BF16), preferred_element_type=F32)


def _fox_pv_page(p, pg):
    return lax.dot_general(p, pg[B_HEADS * HD:2 * B_HEADS * HD, :].astype(BF16), NT_DIMS,
                           preferred_element_type=F32)


def _even_sample_body(pt_ref, qa_ref, qb_ref, dnew_ref, fnew_ref, lnew_ref, lam_ref, *rest, lam_init):
    del pt_ref
    dpages = rest[0:N_PAGES]
    fpages = rest[N_PAGES:2 * N_PAGES]
    lpages = rest[2 * N_PAGES:3 * N_PAGES]
    o_ref = rest[3 * N_PAGES]
    nt = dnew_ref.shape[0]
    nrow = qa_ref.shape[0]
    w = qa_ref.shape[1]
    rid = lax.broadcasted_iota(jnp.int32, (nrow, 1), 0)
    t_row = rid // 8
    h_row = rid % 8
    lane = lax.broadcasted_iota(jnp.int32, (nrow, w), 1)

    res = _paged_attend(qa_ref, dpages, _diff_score_page, _diff_pv_page, dnew_ref, None, None, t_row)
    lam = _lambda_value(lam_ref, lam_init)
    coef = jnp.where(h_row % 2 == 0, 1.0, -lam)
    wgt = jnp.where(lane // (2 * HD) == h_row // 2, coef, 0.0)
    o_ref[:, 0:w] = jnp.sum((res * wgt).reshape(nt, 8, w), axis=1)

    lt = jnp.concatenate([pg[...] for pg in lpages], axis=1)
    cum = _cumsum_lanes(lt)
    tot = jnp.concatenate([jnp.sum(lt, axis=1, keepdims=True)] * nt, axis=0)
    lnew = lnew_ref[...]
    tix = lax.broadcasted_iota(jnp.int32, lnew.shape, 0)
    cnew = jnp.zeros_like(lnew)
    for j in range(nt):
        cnew = cnew + jnp.where(tix >= j, lnew[j:j + 1, :], 0.0)
    hsel = lax.broadcasted_iota(jnp.int32, (nrow, B_HEADS), 1) == h_row
    cnew_rows = jnp.broadcast_to(cnew[:, None, :], (nt, 8, B_HEADS)).reshape(nrow, B_HEADS)
    fq_new = jnp.sum(jnp.where(hsel, cnew_rows, 0.0), axis=-1, keepdims=True)
    bias_past = ((tot + fq_new) - jnp.concatenate([cum] * nt, axis=0)) * LOG2E
    bias_new = [(fq_new - jnp.sum(jnp.where(hsel, cnew[j:j + 1, :], 0.0), axis=-1, keepdims=True)) * LOG2E
                for j in range(nt)]
    res = _paged_attend(qb_ref, fpages, _fox_score_page, _fox_pv_page, fnew_ref, bias_past, bias_new, t_row)
    wgt = jnp.where(lane // HD == h_row, 1.0, 0.0)
    o_ref[:, w:2 * w] = jnp.sum((res * wgt).reshape(nt, 8, w), axis=1)


def _even_sample_attn(e, page_table, qbd_a, qbd_b, dkv_new, fkv_new, lnew, lam_p, cache_d, cache_f, cache_l,
                      lam_init):
    nb, nt = dkv_new.shape[0], dkv_new.shape[1]

    def per_b(shape):
        nd = len(shape)
        return pl.BlockSpec((None,) + tuple(shape[1:]), lambda b, pt: (b,) + (0,) * (nd - 1))

    def page(rows, p):
        return pl.BlockSpec((None, None, rows, PAGE_SIZE), lambda b, pt: (e, pt[b, p], 0, 0))

    in_specs = [per_b(qbd_a.shape), per_b(qbd_b.shape), per_b(dkv_new.shape), per_b(fkv_new.shape),
                per_b(lnew.shape), pl.BlockSpec(lam_p.shape, lambda b, pt: (0, 0))]
    in_specs += [page(cache_d.shape[2], p) for p in range(N_PAGES)]
    in_specs += [page(cache_f.shape[2], p) for p in range(N_PAGES)]
    in_specs += [page(B_HEADS, p) for p in range(N_PAGES)]
    gs = pltpu.PrefetchScalarGridSpec(
        num_scalar_prefetch=1, grid=(nb,), in_specs=in_specs,
        out_specs=pl.BlockSpec((None, nt, 1024), lambda b, pt: (b, 0, 0)))
    return pl.pallas_call(
        functools.partial(_even_sample_body, lam_init=lam_init),
        out_shape=jax.ShapeDtypeStruct((nb, nt, 1024), F32), grid_spec=gs,
        compiler_params=_cparams(("arbitrary",)), name="even_sample")(
            page_table, qbd_a, qbd_b, dkv_new, fkv_new, lnew, lam_p,
            *([cache_d] * N_PAGES), *([cache_f] * N_PAGES), *([cache_l] * N_PAGES))


def _nsa_prompt_body(q_ref, gate_ref, kc_ref, kv_ref, oh_ref, o_ref, *, nblk):
    nq = q_ref.shape[1]
    s_len = kv_ref.shape[0]
    n_cmp = kc_ref.shape[0]
    n_sel = s_len // SEL_BLOCK
    p = pl.program_id(1)
    blocks = (p, nblk - 1 - p)
    nrow = D_REP * nq
    wq = D_HEADS * HD
    lo_q = lax.broadcasted_iota(jnp.int32, (nq, LANES), 1) < HALF
    row_q = lax.broadcasted_iota(jnp.int32, (nrow, 1), 0) % nq
    diag_mask = lax.broadcasted_iota(jnp.int32, (nrow, nq), 1) <= row_q
    pr_i = lax.broadcasted_iota(jnp.int32, (n_sel, n_cmp), 0)
    pc_i = lax.broadcasted_iota(jnp.int32, (n_sel, n_cmp), 1)
    pair_t = jnp.where(pc_i // (SEL_BLOCK // CMP_BLOCK) == pr_i, 1.0, 0.0)
    blk = lax.broadcasted_iota(jnp.int32, (n_sel, nq), 0)
    qlane = lax.broadcasted_iota(jnp.int32, (n_sel, nq), 1)
    wlen = WINDOW + nq

    def stack_q(i, base, g):
        parts = []
        for rep in range(D_REP):
            hd = g * D_REP + rep
            slab = q_ref[i, :, base + (hd // 2) * LANES:base + (hd // 2 + 1) * LANES]
            keep = lo_q if hd % 2 == 0 else jnp.logical_not(lo_q)
            parts.append(jnp.where(keep, slab, jnp.zeros_like(slab)))
        return jnp.concatenate(parts, axis=0)

    for g in range(D_KV):
        kc = kc_ref[:, g * LANES:(g + 1) * LANES].astype(BF16)
        vc = kc_ref[:, (2 + g) * LANES:(3 + g) * LANES].astype(BF16)
        q_aug, o_c, o_w = [], [], []
        for i in range(2):
            q0 = blocks[i] * nq
            qpos_col = q0 + row_q
            qc = stack_q(i, 0, g)
            qr = stack_q(i, wq, g)
            sc = lax.dot_general(qc, kc, NT_DIMS, preferred_element_type=F32)
            c_end = lax.broadcasted_iota(jnp.int32, (nrow, n_cmp), 1) * CMP_BLOCK + (CMP_BLOCK - 1)
            pc = _masked_softmax_rows(sc, c_end <= qpos_col)
            o_c.append(jnp.dot(pc.astype(BF16), vc, preferred_element_type=F32))
            imp_c = pc[0:nq] + pc[nq:2 * nq] + pc[2 * nq:3 * nq] + pc[3 * nq:4 * nq]
            imp_t = lax.dot_general(pair_t, imp_c, NT_DIMS, precision=HIGHEST, preferred_element_type=F32)
            qp = q0 + qlane
            score = jnp.where(qp // SEL_BLOCK == blk, D_REP + 1.0, jnp.where(blk * SEL_BLOCK <= qp, imp_t, -1.0))
            sel_t = _topk_rows(score, min(N_SEL, n_sel))
            pen_t = jnp.concatenate([(sel_t - 1.0) * BIG, jnp.zeros((LANES - n_sel, nq), F32)], axis=0)
            pen = pen_t.T.astype(BF16)
            q_aug.append(jnp.concatenate([qr, jnp.concatenate([pen] * D_REP, axis=0)], axis=1))
            ws = pl.multiple_of(jnp.maximum(q0 - WINDOW, 0), nq)
            wk = kv_ref[pl.ds(ws, wlen), (4 + g) * LANES:(5 + g) * LANES]
            wv = kv_ref[pl.ds(ws, wlen), (6 + g) * LANES:(7 + g) * LANES]
            wpos = ws + lax.broadcasted_iota(jnp.int32, (nrow, wlen), 1)
            wmask = (wpos <= qpos_col) & (wpos > qpos_col - WINDOW)
            sw = jnp.where(wmask, lax.dot_general(qr, wk, NT_DIMS, preferred_element_type=F32), NEG)
            pw = jnp.exp2(sw - jnp.max(sw, axis=-1, keepdims=True))
            inv = 1.0 / jnp.maximum(jnp.sum(pw, axis=-1, keepdims=True), TINY)
            o_w.append(jnp.dot(pw.astype(BF16), wv, preferred_element_type=F32) * inv)

        def score_fn(qh, kb, g=g):
            ks = pl.multiple_of(kb * nq, nq)
            k_aug = jnp.concatenate([kv_ref[pl.ds(ks, nq), g * LANES:(g + 1) * LANES], oh_ref[pl.ds(ks, nq), :]],
                                    axis=1)
            return lax.dot_general(qh, k_aug, NT_DIMS, preferred_element_type=F32)

        def v_fn(kb, g=g):
            return kv_ref[pl.ds(pl.multiple_of(kb * nq, nq), nq), (2 + g) * LANES:(3 + g) * LANES]

        o_s = _paired_causal_softmax_pv(q_aug, score_fn, v_fn, p, nblk, diag_mask)

        for i in range(2):
            gates = gate_ref[i]
            heads = []
            for rep in range(D_REP):
                hd = g * D_REP + rep
                rows = slice(rep * nq, (rep + 1) * nq)
                heads.append(gates[:, 3 * hd:3 * hd + 1] * o_c[i][rows] + gates[:, 3 * hd + 1:3 * hd + 2] * o_s[i][rows]
                             + gates[:, 3 * hd + 2:3 * hd + 3] * o_w[i][rows])
            for j in range(D_REP // 2):
                slab = g * (D_REP // 2) + j
                o_ref[i, :, slab * LANES:(slab + 1) * LANES] = jnp.where(lo_q, heads[2 * j], heads[2 * j + 1])


def _nsa_prompt(q, gates, kcd, nsa_kv, onehot):
    b, s, _ = nsa_kv.shape
    nq = TOKEN_TILE
    nblk = s // nq
    q4 = q.reshape(b, 2, s // 2, q.shape[-1])
    g4 = gates.reshape(b, 2, s // 2, gates.shape[-1])
    out = pl.pallas_call(
        functools.partial(_nsa_prompt_body, nblk=nblk),
        out_shape=jax.ShapeDtypeStruct((b, 2, s // 2, D_HEADS * HD), F32), grid=(b, nblk // 2),
        in_specs=[pl.BlockSpec((None, 2, nq, q.shape[-1]), lambda bi, p: (bi, 0, p, 0)),
                  pl.BlockSpec((None, 2, nq, gates.shape[-1]), lambda bi, p: (bi, 0, p, 0)),
                  pl.BlockSpec((None,) + kcd.shape[1:], lambda bi, p: (bi, 0, 0)),
                  pl.BlockSpec((None, s, nsa_kv.shape[-1]), lambda bi, p: (bi, 0, 0)),
                  pl.BlockSpec(onehot.shape, lambda bi, p: (0, 0))],
        out_specs=pl.BlockSpec((None, 2, nq, D_HEADS * HD), lambda bi, p: (bi, 0, p, 0)),
        compiler_params=_cparams(("arbitrary", "arbitrary")), name="nsa_prompt")(q4, g4, kcd, nsa_kv, onehot)
    return out.reshape(b * s, D_HEADS * HD)


def _nsa_sample_body(pt_ref, qc_ref, qr_ref, gate_ref, snew_ref, wnew_ref, wst_ref, avg_ref, *rest):
    del pt_ref
    o_ref = rest[-1]
    rows = []
    for j in range(NSA_DECODE_ROWS):
        pages = rest[2 * N_PAGES * j:2 * N_PAGES * (j + 1)]
        rows.append(_nsa_sample_row(qc_ref.at[j], qr_ref.at[j], gate_ref.at[j], snew_ref.at[j], wnew_ref.at[j],
                                    wst_ref.at[j], avg_ref, pages[0:N_PAGES], pages[N_PAGES:2 * N_PAGES],
                                    o_ref.at[j]))
    while rows:
        rows = [r for r in rows if next(r, True) is None]


def _nsa_sample_row(qc_ref, qr_ref, gate_ref, snew_ref, wnew_ref, wst_ref, avg_ref, cpages, spages, o_ref):
    nrow = qc_ref.shape[0]
    nt = snew_ref.shape[0]
    rid = lax.broadcasted_iota(jnp.int32, (nrow, 1), 0)
    t_row = rid // D_HEADS
    qc = qc_ref[...]
    qr = qr_ref[...]
    qrf = qr.astype(F32)

    cmp_t = jnp.concatenate([pg[...].astype(BF16) for pg in cpages], axis=1)
    means_t = jnp.dot(cmp_t, avg_ref[...], preferred_element_type=F32)
    kc_t = means_t[0:LANES].astype(BF16)
    vc_t = means_t[LANES:2 * LANES].astype(BF16)
    sc = jnp.dot(qc, kc_t, preferred_element_type=F32)
    mc = jnp.max(sc, axis=-1, keepdims=True)
    pc = jnp.exp2(sc - mc)
    pc = pc / jnp.maximum(jnp.sum(pc, axis=-1, keepdims=True), TINY)
    o_c = lax.dot_general(pc.astype(BF16), vc_t, NT_DIMS, preferred_element_type=F32)
    yield

    ncol = nt * D_KV
    gr = lax.broadcasted_iota(jnp.int32, (ncol, nrow), 0)
    gc = lax.broadcasted_iota(jnp.int32, (ncol, nrow), 1)
    gsum = jnp.where(gc // D_REP == gr, 1.0, 0.0)
    imp_c = jnp.dot(gsum, pc, precision=HIGHEST, preferred_element_type=F32)
    pr = lax.broadcasted_iota(jnp.int32, (N_PAST_SEL, N_PAST_CMP), 0)
    pcx = lax.broadcasted_iota(jnp.int32, (N_PAST_SEL, N_PAST_CMP), 1)
    pair_t = jnp.where(pcx // (SEL_BLOCK // CMP_BLOCK) == pr, 1.0, 0.0)
    imp_t = lax.dot_general(pair_t, imp_c, NT_DIMS, precision=HIGHEST, preferred_element_type=F32)
    sel_t = _topk_rows(imp_t, N_SEL - 1).astype(BF16)
    er = lax.broadcasted_iota(jnp.int32, (nrow, ncol), 0)
    ec = lax.broadcasted_iota(jnp.int32, (nrow, ncol), 1)
    g_t = jnp.where(er // D_REP == ec, 1.0, 0.0).astype(BF16)
    sel_r = lax.dot_general(g_t, sel_t, NT_DIMS, preferred_element_type=F32)
    eb_r = lax.broadcasted_iota(jnp.int32, (N_PAST_SEL, PAST_LEN), 0)
    eb_c = lax.broadcasted_iota(jnp.int32, (N_PAST_SEL, PAST_LEN), 1)
    eb = jnp.where(eb_c // SEL_BLOCK == eb_r, 1.0, 0.0).astype(BF16)
    smask = jnp.dot(sel_r.astype(BF16), eb, preferred_element_type=F32) > 0.5
    yield

    def new_rows(s_past, mask_past, new_ref):
        s_past = jnp.where(mask_past, s_past, NEG)
        s_new = [jnp.where(t_row >= j, jnp.sum(qrf * new_ref[j:j + 1, 0:LANES], axis=-1, keepdims=True), NEG)
                 for j in range(nt)]
        m = jnp.max(s_past, axis=-1, keepdims=True)
        for sj in s_new:
            m = jnp.maximum(m, sj)
        p = jnp.where(mask_past, jnp.exp2(s_past - m), 0.0)
        p_new = [jnp.where(t_row >= j, jnp.exp2(s_new[j] - m), 0.0) for j in range(nt)]
        l = jnp.sum(p, axis=-1, keepdims=True)
        for pj in p_new:
            l = l + pj
        inv = 1.0 / jnp.maximum(l, TINY)
        return p * inv, [pj * inv for pj in p_new]

    s_parts = [jnp.dot(qr, pg[0:LANES, :].astype(BF16), preferred_element_type=F32) for pg in spages]
    ps, ps_new = new_rows(jnp.concatenate(s_parts, axis=1), smask, snew_ref)
    psb = ps.astype(BF16)
    o_s = None
    for i, pg in enumerate(spages):
        part = lax.dot_general(psb[:, i * PAGE_SIZE:(i + 1) * PAGE_SIZE], pg[LANES:2 * LANES, :].astype(BF16),
                               NT_DIMS, preferred_element_type=F32)
        o_s = part if o_s is None else o_s + part
    for j in range(nt):
        o_s = o_s + ps_new[j] * snew_ref[j:j + 1, LANES:2 * LANES]
    yield

    wb = wst_ref.shape[1]
    wk_t = wst_ref[0:LANES, :].astype(BF16)
    wv_t = wst_ref[LANES:2 * LANES, :].astype(BF16)
    jpos = lax.broadcasted_iota(jnp.int32, (nrow, wb), 1)
    wmask = jpos > t_row - (WINDOW - wb)
    pw, pw_new = new_rows(jnp.dot(qr, wk_t, preferred_element_type=F32), wmask, wnew_ref)
    o_w = lax.dot_general(pw.astype(BF16), wv_t, NT_DIMS, preferred_element_type=F32)
    for j in range(nt):
        o_w = o_w + pw_new[j] * wnew_ref[j:j + 1, LANES:2 * LANES]

    g = gate_ref[...]
    o_ref[...] = g[:, 0:1] * o_c + g[:, 1:2] * o_s + g[:, 2:3] * o_w


def _nsa_sample_attn(o, page_table, qc_bd, qr_bd, gates, snew, wnew, win_state, cache_c, cache_s):
    nb = qc_bd.shape[0]
    rows = NSA_DECODE_ROWS
    kv2 = 2 * D_KV * HD
    key = jnp.arange(PAST_LEN)[:, None] // CMP_BLOCK
    avg = jnp.where(key == jnp.arange(N_PAST_CMP)[None, :], 1.0 / CMP_BLOCK, 0.0).astype(BF16)

    def per_b(shape):
        nd = len(shape)
        return pl.BlockSpec((rows,) + tuple(shape[1:]), lambda b, pt: (b,) + (0,) * (nd - 1))

    def page(j, p):
        return pl.BlockSpec((None, None, kv2, PAGE_SIZE), lambda b, pt: (o, pt[b * rows + j, p], 0, 0))

    in_specs = [per_b(qc_bd.shape), per_b(qr_bd.shape), per_b(gates.shape), per_b(snew.shape), per_b(wnew.shape),
                pl.BlockSpec((None, rows) + win_state.shape[2:], lambda b, pt: (o, b, 0, 0)),
                pl.BlockSpec(avg.shape, lambda b, pt: (0, 0))]
    page_args = []
    for j in range(rows):
        in_specs += [page(j, p) for p in range(N_PAGES)] + [page(j, p) for p in range(N_PAGES)]
        page_args += [cache_c] * N_PAGES + [cache_s] * N_PAGES
    gs = pltpu.PrefetchScalarGridSpec(
        num_scalar_prefetch=1, grid=(nb // rows,), in_specs=in_specs,
        out_specs=pl.BlockSpec((rows,) + qc_bd.shape[1:], lambda b, pt: (b, 0, 0)))
    return pl.pallas_call(
        _nsa_sample_body, out_shape=jax.ShapeDtypeStruct(qc_bd.shape, F32), grid_spec=gs,
        compiler_params=_cparams(("arbitrary",)), name="nsa_sample")(
            page_table, qc_bd, qr_bd, gates, snew, wnew, win_state, avg, *page_args)


def _rope_tables(pos):
    half = ROT_DIM // 2
    inv = 1.0 / (ROPE_THETA ** (jnp.arange(half, dtype=F32) * 2.0 / ROT_DIM))
    ang = pos.astype(F32)[:, None] * inv[None, :]
    cos, sin = jnp.cos(ang), jnp.sin(ang)
    n = pos.shape[0]
    zeros8 = jnp.zeros((n, half), F32)
    rest0 = jnp.zeros((n, HD - ROT_DIM), F32)
    c = jnp.concatenate([cos, cos, jnp.ones((n, HD - ROT_DIM), F32)], axis=1)
    s1 = jnp.concatenate([-sin, zeros8, rest0], axis=1)
    s2 = jnp.concatenate([zeros8, sin, rest0], axis=1)
    return tuple(jnp.concatenate([t, t], axis=1) for t in (c, s1, s2))


def _block_diag_rows(q, group):
    nb, nt, w = q.shape
    keep = (jnp.arange(w)[None, :] // HD) == jnp.arange(8)[:, None]
    return jnp.where(keep[None, None], q[:, :, None, :], jnp.zeros((), q.dtype)).reshape(nb, nt * 8, w)


def _nsa_rows(q):
    nb, nt, _ = q.shape
    qh = q.reshape(nb, nt, D_HEADS, HD)
    z = jnp.zeros_like(qh)
    first = (jnp.arange(D_HEADS) < D_REP)[None, None, :, None]
    out = jnp.concatenate([jnp.where(first, qh, z), jnp.where(first, z, qh)], axis=-1)
    return out.reshape(nb, nt * D_HEADS, 2 * HD)


def kernel(x_prompt, x_sample, cache_diff_kv, cache_fox_kv, cache_fox_logf, cache_nsa_cmp_kv, cache_nsa_sel_kv,
           state_nsa_win_kv, page_table, norm_g, ffn_w_in, ffn_w_out, even_w_in, even_w_out, fox_b_f, diff_lambda,
           diff_subln_g, odd_w_in, odd_w_out, gmlp_ln_g, gmlp_ln_b, gmlp_w_s, gmlp_b_s):
    bsz, seq, _ = x_prompt.shape
    nb, nt, _ = x_sample.shape
    tp = bsz * seq
    ts = nb * nt
    n_pool = cache_diff_kv.shape[1]

    yp = x_prompt.reshape(tp, D_MODEL)
    ys = x_sample.reshape(ts, D_MODEL)
    tab_p = _rope_tables(jnp.arange(seq))
    tab_s = _rope_tables(PAST_LEN + (jnp.arange(ts) % nt))
    win_bf = ffn_w_in.astype(BF16)
    wout_bf = ffn_w_out.astype(BF16)
    nblk_p = seq // TOKEN_TILE
    sel_onehot = (jnp.arange(seq)[:, None] // SEL_BLOCK == jnp.arange(LANES)[None, :]).astype(BF16)

    kv2 = 2 * D_KV * HD
    keys_last = (0, 1, 3, 4, 5, 2)
    cd = cache_diff_kv.reshape(cache_diff_kv.shape[0], n_pool, PAGE_SIZE * 2 * A_HEADS, 2 * HD)
    cf = cache_fox_kv.transpose(keys_last).reshape(cache_fox_kv.shape[0], n_pool, 2 * B_HEADS * HD, PAGE_SIZE)
    cl = cache_fox_logf.transpose(0, 1, 3, 2)
    cc = cache_nsa_cmp_kv.transpose(keys_last).reshape(cache_nsa_cmp_kv.shape[0], n_pool, kv2, PAGE_SIZE)
    cs = cache_nsa_sel_kv.transpose(keys_last).reshape(cache_nsa_sel_kv.shape[0], n_pool, kv2, PAGE_SIZE)
    wstate = state_nsa_win_kv.transpose(keys_last).reshape(state_nsa_win_kv.shape[0], nb, kv2,
                                                           state_nsa_win_kv.shape[2])

    w512 = 512
    outs = {k: [] for k in ("dkv_s", "fkv_s", "fl_p", "fl_s", "ckv_s", "skv_s", "wkv_s", "gv_s")}
    even_stack = odd_stack = None
    for i in range(DEPTH):
        nrm = norm_g[i]
        if i % 2 == 0:
            e = i // 2
            lam_init = 0.8 - 0.6 * math.exp(-0.3 * i)
            w = even_w_in[e]
            aq, ak, av, bq, bk, bv, bfc = jnp.split(w, [512, 1024, 1536, 2048, 2560, 3072], axis=1)
            wp = jnp.concatenate([aq, bq, ak, av, bk, bv, bfc,
                                  jnp.zeros((D_MODEL, EVEN_COLS_PAD - w.shape[1]), F32)], axis=1).astype(BF16)
            bf_pad = jnp.concatenate([fox_b_f[e], jnp.zeros((LANES - B_HEADS,), F32)])[None, :]
            lam_p = diff_lambda[e]
            sg = diff_subln_g[e][None, :]
            wmo = even_w_out[e].astype(BF16)

            hp, q, dkv, fkv, kb, vb, logf = _even_pre(yp, nrm, win_bf, wout_bf, i, wp, bf_pad, tab_p, nblk=nblk_p,
                                                      prev=even_stack)
            even_stack = (dkv, fkv)
            k3, v3 = (a.reshape(bsz, seq, 1024) for a in (kb, vb))
            oa = _pair_attn_prompt(q, k3, v3, lam_p, fox=False, lam_init=lam_init)
            logf_t = logf.reshape(bsz, seq, B_HEADS).transpose(0, 2, 1).reshape(bsz * B_HEADS, seq)
            frow = _cumsum_rows_of_lanes(logf_t).reshape(bsz, B_HEADS // 2, 2, seq)
            ob = _pair_attn_prompt(q, k3, v3, frow, fox=True)
            yp = _even_post(oa, ob, sg, wmo, hp, nrm, win_bf, wout_bf, i, 1.0 - lam_init, nblk=nblk_p)
            outs["fl_p"].append(logf.reshape(bsz, seq, B_HEADS))

            hs, q, dkv, fkv, kb, vb, logf = _even_pre(ys, nrm, win_bf, wout_bf, i, wp, bf_pad, tab_s)
            qs3 = q.reshape(nb, nt, 1024)
            o = _even_sample_attn(e, page_table, _block_diag_rows(qs3[:, :, :w512], 8),
                                  _block_diag_rows(qs3[:, :, w512:], 8), dkv.reshape(nb, nt, 1024),
                                  fkv.reshape(nb, nt, 1024), logf.reshape(nb, nt, B_HEADS), lam_p, cd, cf,
                                  cl, lam_init).reshape(ts, 1024)
            ys = _even_post(o[:, :w512], o[:, w512:], sg, wmo, hs, nrm, win_bf, wout_bf, i, 1.0 - lam_init)
            outs["dkv_s"].append(dkv.reshape(nb, nt, 2, A_HEADS, 2 * HD))
            outs["fkv_s"].append(fkv.reshape(nb, nt, 2, B_HEADS, HD))
            outs["fl_s"].append(logf.reshape(nb, nt, B_HEADS))
        else:
            o_idx = i // 2
            w = odd_w_in[o_idx]
            wp = jnp.concatenate([w, jnp.zeros((D_MODEL, ODD_COLS_PAD - w.shape[1]), F32)], axis=1).astype(BF16)
            lng = gmlp_ln_g[o_idx][None, :]
            lnb = gmlp_ln_b[o_idx][None, :]
            wmo = odd_w_out[o_idx].astype(BF16)
            ws = gmlp_w_s[o_idx]
            bs = gmlp_b_s[o_idx]
            tm = TOKEN_TILE
            kv_shape = (2, D_KV, HD)

            hp, u, v, q, kcd, ckv, skv, wkv, nsa_kv, gates = _odd_pre(yp, nrm, win_bf, wout_bf, i, wp, lng, lnb, tab_p,
                                                                      nblk=nblk_p, prev=odd_stack)
            odd_stack = (ckv, skv, wkv)
            od = _nsa_prompt(q, gates, kcd.reshape(bsz, seq // CMP_BLOCK, 4 * LANES),
                             nsa_kv.reshape(bsz, seq, 1024), sel_onehot)
            bs_p = jnp.tile(jnp.repeat(bs.T, C_GW, axis=1), (tm // CHUNK, 1))
            yp = _odd_post(u, v, od, ws, bs_p, wmo, hp, nrm, win_bf, wout_bf, i, True, nblk=nblk_p)

            hs, u, v, q, ckv, skv, wkv, gates = _odd_pre(ys, nrm, win_bf, wout_bf, i, wp, lng, lnb, tab_s)
            q3 = q.reshape(nb, nt, 1024)
            kv2 = 2 * D_KV * HD
            od_raw = _nsa_sample_attn(o_idx, page_table, _nsa_rows(q3[:, :, :512]), _nsa_rows(q3[:, :, 512:]),
                                      gates.reshape(nb, nt * D_HEADS, 3), skv.reshape(nb, nt, kv2),
                                      wkv.reshape(nb, nt, kv2), wstate, cc, cs)
            od4 = od_raw.reshape(nb, nt, D_HEADS, 2, HD)
            first = (jnp.arange(D_HEADS) < D_REP)[None, None, :, None]
            od = jnp.where(first, od4[:, :, :, 0], od4[:, :, :, 1]).reshape(ts, 512)
            tpos = jnp.arange(nt)
            coefs = []
            for k in range(nt):
                val = jnp.where((tpos >= k)[None, :], ws[:, tpos, jnp.maximum(tpos - k, 0)], 0.0)
                coefs.append(jnp.tile(jnp.repeat(val.T, C_GW, axis=1), (tm // nt, 1)))
            bs_s = jnp.tile(jnp.repeat(bs[:, :nt].T, C_GW, axis=1), (tm // nt, 1))
            ys = _odd_post(u, v, od, jnp.stack(coefs), bs_s, wmo, hs, nrm, win_bf, wout_bf, i, False)
            outs["ckv_s"].append(ckv.reshape((nb, nt) + kv_shape))
            outs["skv_s"].append(skv.reshape((nb, nt) + kv_shape))
            outs["wkv_s"].append(wkv.reshape((nb, nt) + kv_shape))
            outs["gv_s"].append(v.reshape(nb, nt, C_CH))

    st = lambda k: jnp.stack(outs[k], axis=0)
    n_even, n_odd = (DEPTH + 1) // 2, DEPTH // 2
    tokens_first = (0, 1, 5, 2, 3, 4)
    dkv_p = even_stack[0].reshape(n_even, bsz, seq, 2, A_HEADS, 2 * HD)
    fkv_p = even_stack[1].reshape(n_even, bsz, 2, B_HEADS, HD, seq).transpose(tokens_first)
    win_rows = min(WINDOW, seq)
    ckv_p, skv_p, wkv_p = (a.reshape(n_odd, bsz, 2, D_KV, HD, seq).transpose(tokens_first) for a in odd_stack)
    return (yp.reshape(bsz, seq, D_MODEL), ys.reshape(nb, nt, D_MODEL), dkv_p, st("dkv_s"), fkv_p,
            st("fkv_s"), st("fl_p"), st("fl_s"), ckv_p, st("ckv_s"), skv_p, st("skv_s"), wkv_p[:, :, seq - win_rows:],
            st("wkv_s"), st("gv_s"))
```
